```python
import jax, jax.numpy as jnp
from jax import lax
import numpy as np

D_MODEL = 4096
BATCH = 1
SEQ = 16384
DEPTH = 1

CHUNK = 64
SB_HEADS = 16
SB_HEAD_DIM = 128
SB_WIDTH = SB_HEADS * SB_HEAD_DIM
Q_BLOCK = 128
ML_HEADS = 8
ML_V_DIM = 256
ML_QK_DIM = 128
ML_WIDTH = ML_HEADS * ML_V_DIM
ML_QK_WIDTH = ML_HEADS * ML_QK_DIM
CONV_WIDTH = 4
N_GROUPS = 4
EXPERTS_PER_GROUP = 8
N_EXPERTS = N_GROUPS * EXPERTS_PER_GROUP
TOP_K_IN_GROUP = 2
D_EXPERT = 1024
MOE_BLOCK = 256
N_MOD = 6
DN_ALPHA = (2.0 * DEPTH) ** 0.25
DN_BETA = (8.0 * DEPTH) ** -0.25
LN_EPS = 1e-5
IN_SPLITS = (SB_WIDTH, SB_WIDTH, SB_WIDTH, 2 * ML_QK_WIDTH, ML_WIDTH, ML_WIDTH, 2 * ML_HEADS, D_MODEL, D_MODEL)
IN_WIDTH = sum(IN_SPLITS)

kernel_name = 'hybrid_stickbreak_mlstm_hmoe_block'


def _layer_norm(x):
    xf = x.astype(jnp.float32)
    mu = jnp.mean(xf, axis=-1, keepdims=True)
    var = jnp.mean(jnp.square(xf - mu), axis=-1, keepdims=True)
    return ((xf - mu) * lax.rsqrt(var + LN_EPS)).astype(x.dtype)


def _split_heads(t, n_heads):
    return t.reshape(t.shape[0], t.shape[1], n_heads, -1)


def _causal_conv(x, w, b):
    k, ch = w.shape
    y = lax.conv_general_dilated(x, w[:, None, :].astype(x.dtype), window_strides=(1,),
                                 padding=((k - 1, 0),), dimension_numbers=('NWC', 'WIO', 'NWC'),
                                 feature_group_count=ch)
    return y + b


def _stick_breaking_attention(q, k, v):
    seq, dh = q.shape[2], q.shape[3]
    scale = dh ** -0.5
    outs = []
    for blk in range(seq // Q_BLOCK):
        q0 = blk * Q_BLOCK
        kv_len = q0 + Q_BLOCK
        qb = q[:, :, q0:kv_len]
        kb = k[:, :, :kv_len]
        vb = v[:, :, :kv_len]
        z = jnp.einsum('bhqd,bhkd->bhqk', qb, kb, preferred_element_type=jnp.float32) * scale
        strict = jnp.arange(kv_len)[None, :] < (q0 + jnp.arange(Q_BLOCK))[:, None]
        log_keep = jnp.where(strict, jax.nn.log_sigmoid(-z), 0.0)
        between = lax.cumsum(log_keep, axis=3, reverse=True) - log_keep
        w = jnp.where(strict, jnp.exp(jax.nn.log_sigmoid(z) + between), 0.0)
        outs.append(jnp.einsum('bhqk,bhkd->bhqd', w.astype(v.dtype), vb))
    return jnp.concatenate(outs, axis=2)


def _mlstm_step(carry, inp):
    c_state, n_state, m_state = carry
    qc, kc, vc, gc, ac = inp
    inter_num = jnp.einsum('bhvk,bhlk->bhlv', c_state, qc)
    inter_den = jnp.einsum('bhk,bhlk->bhl', n_state, qc)
    m_new = jnp.maximum(gc + m_state, jnp.max(ac, axis=-1))
    decay = jnp.exp(gc + m_state - m_new)
    w = jnp.exp(ac - m_new[..., None])
    c_new = decay[..., None, None] * c_state + jnp.einsum('bhl,bhlv,bhlk->bhvk', w, vc, kc)
    n_new = decay[..., None] * n_state + jnp.einsum('bhl,bhlk->bhk', w, kc)
    return (c_new, n_new, m_new), (inter_num, inter_den, m_state)


def _mlstm_chunkwise(q, k, v, i_pre, f_pre):
    bsz, seq, heads, dk = q.shape
    dv = v.shape[-1]
    nc = seq // CHUNK
    f32 = jnp.float32

    def to_chunks(t):
        t = t.reshape((bsz, nc, CHUNK) + t.shape[2:])
        return jnp.moveaxis(t, 3, 1)

    qc = to_chunks(q).astype(f32)
    kc = to_chunks(k).astype(f32) * (dk ** -0.5)
    vc = to_chunks(v).astype(f32)
    ig = to_chunks(i_pre).astype(f32)
    b = jnp.cumsum(jax.nn.log_sigmoid(to_chunks(f_pre).astype(f32)), axis=-1)
    g_tot = b[..., -1]
    a = g_tot[..., None] - b + ig

    init = (jnp.zeros((bsz, heads, dv, dk), f32), jnp.zeros((bsz, heads, dk), f32),
            jnp.zeros((bsz, heads), f32))
    xs = (jnp.moveaxis(qc, 2, 0), jnp.moveaxis(kc, 2, 0), jnp.moveaxis(vc, 2, 0),
          jnp.moveaxis(g_tot, 2, 0), jnp.moveaxis(a, 2, 0))
    _, (inter_num, inter_den, m_prev) = lax.scan(_mlstm_step, init, xs)
    inter_num = jnp.moveaxis(inter_num, 0, 2)
    inter_den = jnp.moveaxis(inter_den, 0, 2)
    m_prev = jnp.moveaxis(m_prev, 0, 2)

    tril = jnp.tril(jnp.ones((CHUNK, CHUNK), dtype=bool))
    dmat = jnp.where(tril, b[..., :, None] - b[..., None, :] + ig[..., None, :], -jnp.inf)
    m_inter = b + m_prev[..., None]
    m_row = jnp.maximum(m_inter, jnp.max(dmat, axis=-1))
    p = jnp.exp(dmat - m_row[..., None]) * jnp.einsum('bhcld,bhcsd->bhcls', qc, kc)
    inter_scale = jnp.exp(m_inter - m_row)
    num = jnp.einsum('bhcls,bhcsv->bhclv', p, vc) + inter_scale[..., None] * inter_num
    den = jnp.sum(p, axis=-1) + inter_scale * inter_den
    h = num / jnp.maximum(jnp.abs(den), jnp.exp(-m_row))[..., None]
    return jnp.moveaxis(h, 1, 3).reshape(bsz, seq, heads, dv).astype(v.dtype)


def _hierarchical_moe(u, w_rg, b_rg, w_re, b_re, w1, w3, w2):
    bsz, seq, d = u.shape
    n_tok = bsz * seq
    xt = u.reshape(n_tok, d)
    g_logits = (xt @ w_rg).astype(jnp.float32) + b_rg
    g_prob = jax.nn.softmax(g_logits, axis=-1)
    g_idx = jnp.argmax(g_logits, axis=-1)
    g_w = jnp.take_along_axis(g_prob, g_idx[:, None], axis=1)[:, 0]
    e_logits = ((xt @ w_re).astype(jnp.float32) + b_re).reshape(n_tok, N_GROUPS, EXPERTS_PER_GROUP)
    e_logits = jnp.take_along_axis(e_logits, g_idx[:, None, None], axis=1)[:, 0]
    top_v, top_i = lax.top_k(e_logits, TOP_K_IN_GROUP)
    top_w = jax.nn.softmax(top_v, axis=-1) * g_w[:, None]

    expert_id = (g_idx[:, None] * EXPERTS_PER_GROUP + top_i).reshape(-1)
    tok_id = jnp.repeat(jnp.arange(n_tok), TOP_K_IN_GROUP)
    gate = top_w.reshape(-1)
    order = jnp.argsort(expert_id)
    e_sorted = expert_id[order]
    tok_sorted = tok_id[order]
    gate_sorted = gate[order]

    counts = jnp.bincount(expert_id, length=N_EXPERTS)
    padded = ((counts + MOE_BLOCK - 1) // MOE_BLOCK) * MOE_BLOCK
    start = jnp.cumsum(counts) - counts
    pend = jnp.cumsum(padded)
    pstart = pend - padded
    n_assign = n_tok * TOP_K_IN_GROUP
    dest = pstart[e_sorted] + (jnp.arange(n_assign) - start[e_sorted])
    n_blocks = -(-n_assign // MOE_BLOCK) + N_EXPERTS
    buf = jnp.zeros((n_blocks * MOE_BLOCK, d), u.dtype).at[dest].set(xt[tok_sorted])
    block_expert = jnp.clip(jnp.searchsorted(pend, jnp.arange(n_blocks) * MOE_BLOCK, side='right'),
                            0, N_EXPERTS - 1)

    def expert_block(args):
        xb, e = args
        hid = jax.nn.silu(xb @ w1[e]) * (xb @ w3[e])
        return hid @ w2[e]

    yb = lax.map(expert_block, (buf.reshape(n_blocks, MOE_BLOCK, d), block_expert))
    y_assign = yb.reshape(-1, d)[dest] * gate_sorted[:, None].astype(yb.dtype)
    out = jax.ops.segment_sum(y_assign, tok_sorted, num_segments=n_tok)
    return out.reshape(bsz, seq, d)


def setup_inputs(seed: int = 0) -> dict:
    key = jax.random.key(seed)
    ks = jax.random.split(key, 24)
    f32 = jnp.float32
    L = DEPTH

    def nrm(k, shape, s):
        return jax.random.normal(k, shape, f32) * s

    x = nrm(ks[0], (BATCH, SEQ, D_MODEL), 1.0)
    c = nrm(ks[1], (BATCH, D_MODEL), 1.0)
    w_ada = nrm(ks[2], (L, D_MODEL, N_MOD * D_MODEL), D_MODEL ** -0.5)
    b_ada = nrm(ks[3], (L, N_MOD * D_MODEL), 0.02)
    col_scale = jnp.concatenate([
        jnp.full((2 * SB_WIDTH,), 1.0, f32), jnp.full((SB_WIDTH,), DN_BETA, f32),
        jnp.full((2 * ML_QK_WIDTH,), 1.0, f32), jnp.full((ML_WIDTH,), DN_BETA, f32),
        jnp.full((ML_WIDTH + 2 * ML_HEADS + 2 * D_MODEL,), 1.0, f32)]) * (D_MODEL ** -0.5)
    w_in = jax.random.normal(ks[4], (L, D_MODEL, IN_WIDTH), f32) * col_scale
    b_if = jnp.concatenate([nrm(ks[5], (L, ML_HEADS), 0.1),
                            jax.random.uniform(ks[6], (L, ML_HEADS), f32, 3.0, 6.0)], axis=-1)
    conv_w = nrm(ks[7], (L, CONV_WIDTH, 2 * ML_QK_WIDTH), CONV_WIDTH ** -0.5)
    conv_b = nrm(ks[8], (L, 2 * ML_QK_WIDTH), 0.02)
    head_norm_g = 1.0 + nrm(ks[9], (L, ML_WIDTH), 0.02)
    w_branch_sb = nrm(ks[10], (L, SB_WIDTH, D_MODEL), DN_BETA * SB_WIDTH ** -0.5)
    w_branch_ml = nrm(ks[11], (L, ML_WIDTH, D_MODEL), DN_BETA * ML_WIDTH ** -0.5)
    w_out = nrm(ks[12], (L, D_MODEL, D_MODEL), DN_BETA * D_MODEL ** -0.5)
    ln1_g = 1.0 + nrm(ks[13], (L, D_MODEL), 0.02)
    ln1_b = nrm(ks[14], (L, D_MODEL), 0.02)
    w_rg = nrm(ks[15], (L, D_MODEL, N_GROUPS), D_MODEL ** -0.5)
    b_rg = nrm(ks[16], (L, N_GROUPS), 0.01)
    w_re = nrm(ks[17], (L, D_MODEL, N_EXPERTS), D_MODEL ** -0.5)
    b_re = nrm(ks[18], (L, N_EXPERTS), 0.01)
    w1 = nrm(ks[19], (L, N_EXPERTS, D_MODEL, D_EXPERT), DN_BETA * D_MODEL ** -0.5)
    w3 = nrm(ks[20], (L, N_EXPERTS, D_MODEL, D_EXPERT), DN_BETA * D_MODEL ** -0.5)
    w2 = nrm(ks[21], (L, N_EXPERTS, D_EXPERT, D_MODEL), DN_BETA * D_EXPERT ** -0.5)
    ln2_g = 1.0 + nrm(ks[22], (L, D_MODEL), 0.02)
    ln2_b = nrm(ks[23], (L, D_MODEL), 0.02)
    return {'x': x, 'c': c, 'w_ada': w_ada, 'b_ada': b_ada, 'w_in': w_in, 'b_if': b_if,
            'conv_w': conv_w, 'conv_b': conv_b, 'head_norm_g': head_norm_g,
            'w_branch_sb': w_branch_sb, 'w_branch_ml': w_branch_ml, 'w_out': w_out,
            'ln1_g': ln1_g, 'ln1_b': ln1_b, 'w_rg': w_rg, 'b_rg': b_rg, 'w_re': w_re, 'b_re': b_re,
            'w1': w1, 'w3': w3, 'w2': w2, 'ln2_g': ln2_g, 'ln2_b': ln2_b}


def reference(x, c, w_ada, b_ada, w_in, b_if, conv_w, conv_b, head_norm_g,
              w_branch_sb, w_branch_ml, w_out, ln1_g, ln1_b, w_rg, b_rg, w_re, b_re,
              w1, w3, w2, ln2_g, ln2_b):
    bsz, seq = x.shape[0], x.shape[1]
    split_points = np.cumsum(IN_SPLITS)[:-1].tolist()
    for l in range(DEPTH):
        mod = jax.nn.silu(c) @ w_ada[l] + b_ada[l]
        shift1, scale1, gate1, shift2, scale2, gate2 = [m[:, None, :] for m in jnp.split(mod, N_MOD, axis=-1)]

        u1 = _layer_norm(x) * (1.0 + scale1) + shift1
        proj = u1 @ w_in[l]
        q_sb, k_sb, v_sb, qk_ml, v_ml, o_ml, if_ml, g_sb, g_ml = jnp.split(proj, split_points, axis=-1)

        y_sb = _stick_breaking_attention(_split_heads(q_sb, SB_HEADS).transpose(0, 2, 1, 3),
                                         _split_heads(k_sb, SB_HEADS).transpose(0, 2, 1, 3),
                                         _split_heads(v_sb, SB_HEADS).transpose(0, 2, 1, 3))
        y_sb = y_sb.transpose(0, 2, 1, 3).reshape(bsz, seq, SB_WIDTH)

        qk_ml = jax.nn.silu(_causal_conv(qk_ml, conv_w[l], conv_b[l]))
        gates_if = if_ml + b_if[l]
        h_ml = _mlstm_chunkwise(_split_heads(qk_ml[..., :ML_QK_WIDTH], ML_HEADS),
                                _split_heads(qk_ml[..., ML_QK_WIDTH:], ML_HEADS),
                                _split_heads(v_ml, ML_HEADS),
                                gates_if[..., :ML_HEADS], gates_if[..., ML_HEADS:])
        h_ml = (_layer_norm(h_ml) * head_norm_g[l].reshape(ML_HEADS, ML_V_DIM)).reshape(bsz, seq, ML_WIDTH)
        h_ml = jax.nn.sigmoid(o_ml) * h_ml

        merged = (jax.nn.sigmoid(g_sb) * (y_sb @ w_branch_sb[l])
                  + jax.nn.sigmoid(g_ml) * (h_ml @ w_branch_ml[l]))
        mix = merged @ w_out[l]
        x = _layer_norm(DN_ALPHA * x + gate1 * mix) * ln1_g[l] + ln1_b[l]

        u2 = _layer_norm(x) * (1.0 + scale2) + shift2
        ffn = _hierarchical_moe(u2, w_rg[l], b_rg[l], w_re[l], b_re[l], w1[l], w3[l], w2[l])
        x = _layer_norm(DN_ALPHA * x + gate2 * ffn) * ln2_g[l] + ln2_b[l]
    return x
```

```python
import functools

import numpy as np
import jax
import jax.numpy as jnp
from jax import lax
from jax.experimental import pallas as pl
from jax.experimental.pallas import tpu as pltpu

F32 = jnp.float32
BF16 = jnp.bfloat16
U32 = jnp.uint32
I32 = jnp.int32

DEPTH = 1
CHUNK = 64
SB_HEADS = 16
SB_HEAD_DIM = 128
ML_HEADS = 8
ML_V_DIM = 256
ML_QK_DIM = 128
CONV_WIDTH = 4
N_GROUPS = 4
EXPERTS_PER_GROUP = 8
TOP_K_IN_GROUP = 2
N_MOD = 6
DN_ALPHA = (2.0 * DEPTH) ** 0.25
LN_EPS = 1e-5

LANES = 128
SUBLANES = 8
VMEM_LIMIT_BYTES = 60 * 1024 * 1024

EXP_UNDERFLOW = -104.0

MOE_TILE = 256


def _tile(n, pref, align=LANES):
    if n <= pref:
        return n
    t = (pref // align) * align
    while n % t:
        t -= align
    return t


def _cparams(*sem):
    return pltpu.CompilerParams(dimension_semantics=tuple(sem), vmem_limit_bytes=VMEM_LIMIT_BYTES)


def _ln(x):
    mu = jnp.mean(x, axis=-1, keepdims=True)
    xc = x - mu
    var = jnp.mean(xc * xc, axis=-1, keepdims=True)
    return xc * lax.rsqrt(var + LN_EPS)


def _split2(x):
    hi = x.astype(BF16)
    lo = (x - hi.astype(F32)).astype(BF16)
    return hi, lo


def _split3(x):
    hi = x.astype(BF16)
    r = x - hi.astype(F32)
    mid = r.astype(BF16)
    lo = (r - mid.astype(F32)).astype(BF16)
    return hi, mid, lo


def _dot(a, b):
    return jnp.dot(a, b, preferred_element_type=F32)


def _dot_nt(a, b):
    return lax.dot_general(a, b, (((1,), (1,)), ((), ())), preferred_element_type=F32)


def _dot_tn(a, b):
    return lax.dot_general(a, b, (((0,), (0,)), ((), ())), preferred_element_type=F32)


def _softplus(z):
    return jnp.maximum(z, 0.0) + jnp.log1p(jnp.exp(-jnp.abs(z)))


def _pack_halves(x):
    n = x.shape[1] // 2
    lo = lax.bitcast_convert_type(x[:, :n].astype(BF16).astype(F32), U32)
    hi = lax.bitcast_convert_type(x[:, n:].astype(BF16).astype(F32), U32)
    return (lo >> 16) | (hi & jnp.uint32(0xFFFF0000))


def _unpack_halves(u):
    lo = lax.bitcast_convert_type(u << 16, F32)
    hi = lax.bitcast_convert_type(u & jnp.uint32(0xFFFF0000), F32)
    return lo, hi


def _mod_kernel(c_ref, w_ref, b_ref, o_ref):
    @pl.when(pl.program_id(1) == 0)
    def _():
        o_ref[...] = b_ref[...]

    c = c_ref[...]
    s = c * jax.nn.sigmoid(c)
    o_ref[...] += jnp.sum(w_ref[...] * s, axis=0, keepdims=True)


def _mod(c_col, w_ada, b_ada):
    d, n = w_ada.shape
    tk = _tile(d, 512, SUBLANES)
    tn = _tile(n, 2048)
    return pl.pallas_call(
        _mod_kernel,
        grid=(n // tn, d // tk),
        in_specs=[pl.BlockSpec((tk, 1), lambda j, k: (k, 0)),
                  pl.BlockSpec((tk, tn), lambda j, k: (k, j)),
                  pl.BlockSpec((1, tn), lambda j, k: (0, j))],
        out_specs=pl.BlockSpec((1, tn), lambda j, k: (0, j)),
        out_shape=jax.ShapeDtypeStruct((1, n), F32),
        compiler_params=_cparams("parallel", "arbitrary"),
        name="mod",
    )(c_col, w_ada, b_ada)


def _ln1_kernel(x_ref, sc_ref, sh_ref, wh_ref, wl_ref, wth_ref, wtl_ref, u_ref, if_ref, ift_ref):
    u = _ln(x_ref[...]) * (1.0 + sc_ref[...]) + sh_ref[...]
    u_ref[...] = u.astype(BF16)
    uh, ul = _split2(u)
    if_ref[...] = _dot(uh, wh_ref[...]) + _dot(ul, wh_ref[...]) + _dot(uh, wl_ref[...])
    ift_ref[...] = (_dot_nt(wth_ref[...], uh) + _dot_nt(wth_ref[...], ul) + _dot_nt(wtl_ref[...], uh))


def _ln1(x2, scale, shift, w_if):
    s, d = x2.shape
    tm = min(256, s)
    wpad = jnp.zeros((d, LANES), F32).at[:, :w_if.shape[1]].set(w_if)
    wh, wl = _split2(wpad)
    row = pl.BlockSpec((1, d), lambda i: (0, 0))
    wspec = pl.BlockSpec((d, LANES), lambda i: (0, 0))
    wtspec = pl.BlockSpec((LANES, d), lambda i: (0, 0))
    return pl.pallas_call(
        _ln1_kernel,
        grid=(s // tm,),
        in_specs=[pl.BlockSpec((tm, d), lambda i: (i, 0)), row, row, wspec, wspec, wtspec, wtspec],
        out_specs=[pl.BlockSpec((tm, d), lambda i: (i, 0)),
                   pl.BlockSpec((tm, LANES), lambda i: (i, 0)),
                   pl.BlockSpec((LANES, tm), lambda i: (0, i))],
        out_shape=[jax.ShapeDtypeStruct((s, d), BF16),
                   jax.ShapeDtypeStruct((s, LANES), F32),
                   jax.ShapeDtypeStruct((LANES, s), F32)],
        compiler_params=_cparams("parallel"),
        name="ln1",
    )(x2, scale, shift, wh, wl, wh.T, wl.T)


def _mm_kernel(a_ref, b_ref, o_ref):
    o_ref[...] = _dot(a_ref[...], b_ref[...]).astype(o_ref.dtype)


def _matmul(a, b, out_dtype, name):
    m, k = a.shape
    n = b.shape[1]
    tm = _tile(m, 1024, SUBLANES)
    tn = _tile(n, 1024)
    return pl.pallas_call(
        _mm_kernel,
        grid=(m // tm, n // tn),
        in_specs=[pl.BlockSpec((tm, k), lambda i, j: (i, 0)),
                  pl.BlockSpec((k, tn), lambda i, j: (0, j))],
        out_specs=pl.BlockSpec((tm, tn), lambda i, j: (i, j)),
        out_shape=jax.ShapeDtypeStruct((m, n), out_dtype),
        compiler_params=_cparams("parallel", "arbitrary"),
        name=name,
    )(a, b)


def _attn_kernel(q_ref, k_ref, v_ref, o_ref, *, tq, scale):
    qb = pl.program_id(1)
    q = q_ref[...]
    row = lax.broadcasted_iota(I32, (tq, tq), 0)
    col = lax.broadcasted_iota(I32, (tq, tq), 1)
    strict = col < row
    later = (row > col).astype(BF16)

    def tile(kb, c, diagonal):
        start = pl.multiple_of(kb * tq, tq)
        k = k_ref[pl.ds(start, tq), :]
        v = v_ref[pl.ds(start, tq), :]
        z = _dot_nt(q, k) * scale
        log_keep = -_softplus(z)
        if diagonal:
            log_keep = jnp.where(strict, log_keep, 0.0)
        hi, lo = _split2(log_keep)
        between = _dot(hi, later) + _dot(lo, later) + c
        w = jnp.exp(z + log_keep + between)
        if diagonal:
            w = jnp.where(strict, w, 0.0)
        pv = _dot(w.astype(BF16), v)
        return pv, c + jnp.sum(log_keep, axis=1, keepdims=True)

    acc, c = tile(qb, jnp.zeros((tq, 1), F32), True)

    def cond(st):
        j, _, _, cmax = st
        return jnp.logical_and(j <= qb, cmax > EXP_UNDERFLOW)

    def body(st):
        j, acc, c, _ = st
        pv, c = tile(qb - j, c, False)
        return j + 1, acc + pv, c, jnp.max(c)

    _, acc, _, _ = lax.while_loop(cond, body, (jnp.int32(1), acc, c, jnp.max(c)))
    o_ref[...] = acc.astype(o_ref.dtype)


def _attention(proj, q_blk0, k_blk0, v_blk0):
    s = proj.shape[0]
    tq = min(256, s)
    dh = SB_HEAD_DIM
    kern = functools.partial(_attn_kernel, tq=tq, scale=dh ** -0.5)
    return pl.pallas_call(
        kern,
        grid=(SB_HEADS, s // tq),
        in_specs=[pl.BlockSpec((tq, dh), lambda h, i: (i, q_blk0 + h)),
                  pl.BlockSpec((s, dh), lambda h, i: (0, k_blk0 + h)),
                  pl.BlockSpec((s, dh), lambda h, i: (0, v_blk0 + h))],
        out_specs=pl.BlockSpec((tq, dh), lambda h, i: (i, h)),
        out_shape=jax.ShapeDtypeStruct((s, SB_HEADS * dh), BF16),
        compiler_params=_cparams("parallel", "arbitrary"),
        name="attn",
    )(proj, proj, proj)


def _mlstm_kernel(qk_ref, v_ref, o_ref, if_ref, ift_ref, cw_ref, cb_ref, brow_ref, bcol_ref, hg_ref,
                  out_ref, xbuf, ct_ref, nt_ref, m_ref, *, tr):
    heads, dk, dv, L = ML_HEADS, ML_QK_DIM, ML_V_DIM, CHUNK
    qkw = heads * dk
    halo = SUBLANES

    @pl.when(pl.program_id(0) == 0)
    def _():
        xbuf[0:halo, :] = jnp.zeros((halo, 2 * qkw), F32)
        ct_ref[...] = jnp.zeros_like(ct_ref)
        nt_ref[...] = jnp.zeros_like(nt_ref)
        m_ref[...] = jnp.zeros_like(m_ref)

    xbuf[halo:halo + tr, :] = qk_ref[...]
    y = cb_ref[...]
    for i in range(CONV_WIDTH):
        y = y + cw_ref[i:i + 1, :] * xbuf[pl.ds(halo - (CONV_WIDTH - 1) + i, tr), :]
    xbuf[0:halo, :] = xbuf[tr:tr + halo, :]
    qk = y * jax.nn.sigmoid(y)

    g_col = if_ref[...] + brow_ref[...]
    g_row = ift_ref[...] + bcol_ref[...]
    lf_col = -_softplus(-g_col)
    lf_row = -_softplus(-g_row)
    r = lax.broadcasted_iota(I32, (tr, tr), 0)
    s = lax.broadcasted_iota(I32, (tr, tr), 1)
    same_chunk = (r // L) == (s // L)
    incl_col = jnp.logical_and(same_chunk, s <= r).astype(BF16)
    incl_row = jnp.logical_and(same_chunk, r <= s).astype(BF16)
    c1, c2, c3 = _split3(lf_col)
    b_col_all = _dot(incl_col, c1) + _dot(incl_col, c2) + _dot(incl_col, c3)
    r1, r2, r3 = _split3(lf_row)
    b_row_all = _dot(r1, incl_row) + _dot(r2, incl_row) + _dot(r3, incl_row)

    tl = lax.broadcasted_iota(I32, (L, L), 0)
    sl = lax.broadcasted_iota(I32, (L, L), 1)
    tril = sl <= tl

    for ci in range(tr // L):
        r0 = ci * L
        for h in range(heads):
            q = qk[r0:r0 + L, h * dk:(h + 1) * dk].astype(BF16)
            k = (qk[r0:r0 + L, qkw + h * dk:qkw + (h + 1) * dk] * (dk ** -0.5)).astype(BF16)
            v = v_ref[r0:r0 + L, h * dv:(h + 1) * dv]
            ig_col = g_col[r0:r0 + L, h:h + 1]
            b_col = b_col_all[r0:r0 + L, heads + h:heads + h + 1]
            ig_row = g_row[h:h + 1, r0:r0 + L]
            b_row = b_row_all[heads + h:heads + h + 1, r0:r0 + L]
            g_tot = b_row[:, L - 1:L]
            m_prev = m_ref[h][:, 0:1]

            dmat = jnp.where(tril, b_col - b_row + ig_row, -jnp.inf)
            m_inter = b_col + m_prev
            m_row = jnp.maximum(m_inter, jnp.max(dmat, axis=1, keepdims=True))
            p = jnp.exp(dmat - m_row) * _dot_nt(q, k)
            inter_scale = jnp.exp(m_inter - m_row)
            inter_num = _dot(q, ct_ref[h].astype(BF16))
            inter_den = _dot(q, nt_ref[h].astype(BF16))[:, 0:1]
            num = _dot(p.astype(BF16), v) + inter_scale * inter_num
            den = jnp.sum(p, axis=1, keepdims=True) + inter_scale * inter_den
            hh = num / jnp.maximum(jnp.abs(den), jnp.exp(-m_row))

            a_row = g_tot - b_row + ig_row
            a_col = g_tot - b_col + ig_col
            m_new = jnp.maximum(g_tot + m_prev, jnp.max(a_row, axis=1, keepdims=True))
            decay = jnp.exp(g_tot + m_prev - m_new)
            w_col = jnp.exp(a_col - m_new)
            ct_ref[h] = decay * ct_ref[h] + _dot_tn(k, (w_col * v.astype(F32)).astype(BF16))
            nt_ref[h] = decay * nt_ref[h] + _dot_tn(k, jnp.broadcast_to(w_col, (L, LANES)).astype(BF16))
            m_ref[h] = jnp.broadcast_to(m_new, (1, LANES))

            hn = _ln(hh) * hg_ref[:, h * dv:(h + 1) * dv]
            gate = jax.nn.sigmoid(o_ref[r0:r0 + L, h * dv:(h + 1) * dv].astype(F32))
            out_ref[r0:r0 + L, h * dv:(h + 1) * dv] = (gate * hn).astype(out_ref.dtype)


def _mlstm(qk_pre, proj, v_blk, o_blk, if_col, if_row, conv_w, conv_b, b_if, head_norm_g):
    s = qk_pre.shape[0]
    tr = min(256, s)
    heads, dk, dv = ML_HEADS, ML_QK_DIM, ML_V_DIM
    mlw = heads * dv
    qkw2 = 2 * heads * dk
    brow = jnp.zeros((1, LANES), F32).at[0, :2 * heads].set(b_if)
    full = lambda shape: pl.BlockSpec(shape, lambda i: (0,) * len(shape))
    kern = functools.partial(_mlstm_kernel, tr=tr)
    return pl.pallas_call(
        kern,
        grid=(s // tr,),
        in_specs=[pl.BlockSpec((tr, qkw2), lambda i: (i, 0)),
                  pl.BlockSpec((tr, mlw), lambda i: (i, v_blk)),
                  pl.BlockSpec((tr, mlw), lambda i: (i, o_blk)),
                  pl.BlockSpec((tr, LANES), lambda i: (i, 0)),
                  pl.BlockSpec((LANES, tr), lambda i: (0, i)),
                  full((CONV_WIDTH, qkw2)), full((1, qkw2)), full((1, LANES)), full((LANES, 1)),
                  full((1, mlw))],
        out_specs=pl.BlockSpec((tr, mlw), lambda i: (i, 0)),
        out_shape=jax.ShapeDtypeStruct((s, mlw), BF16),
        scratch_shapes=[pltpu.VMEM((tr + 2 * SUBLANES, qkw2), F32),
                        pltpu.VMEM((heads, dk, dv), F32),
                        pltpu.VMEM((heads, dk, LANES), F32),
                        pltpu.VMEM((heads, 1, LANES), F32)],
        compiler_params=_cparams("arbitrary"),
        name="mlstm",
    )(qk_pre, proj, proj, if_col, if_row, conv_w, conv_b.reshape(1, -1), brow, brow.reshape(LANES, 1),
      head_norm_g.reshape(1, -1))


def _merge_kernel(y_ref, h_ref, wsb_ref, wml_ref, gsb_ref, gml_ref, o_ref):
    a = _dot(y_ref[...], wsb_ref[...])
    b = _dot(h_ref[...], wml_ref[...])
    o = jax.nn.sigmoid(gsb_ref[...].astype(F32)) * a + jax.nn.sigmoid(gml_ref[...].astype(F32)) * b
    o_ref[...] = o.astype(o_ref.dtype)


def _merge(y_sb, h_ml, w_sb, w_ml, proj, gsb_col0, gml_col0):
    s = y_sb.shape[0]
    d = w_sb.shape[1]
    tm = _tile(s, 512, SUBLANES)
    tn = _tile(d, 1024)
    gsb0, gml0 = gsb_col0 // tn, gml_col0 // tn
    assert gsb0 * tn == gsb_col0 and gml0 * tn == gml_col0
    return pl.pallas_call(
        _merge_kernel,
        grid=(s // tm, d // tn),
        in_specs=[pl.BlockSpec((tm, y_sb.shape[1]), lambda i, j: (i, 0)),
                  pl.BlockSpec((tm, h_ml.shape[1]), lambda i, j: (i, 0)),
                  pl.BlockSpec((w_sb.shape[0], tn), lambda i, j: (0, j)),
                  pl.BlockSpec((w_ml.shape[0], tn), lambda i, j: (0, j)),
                  pl.BlockSpec((tm, tn), lambda i, j: (i, gsb0 + j)),
                  pl.BlockSpec((tm, tn), lambda i, j: (i, gml0 + j))],
        out_specs=pl.BlockSpec((tm, tn), lambda i, j: (i, j)),
        out_shape=jax.ShapeDtypeStruct((s, d), BF16),
        compiler_params=_cparams("parallel", "arbitrary"),
        name="merge",
    )(y_sb, h_ml, w_sb, w_ml, proj, proj)


def _outproj_kernel(m_ref, w_ref, x_ref, g1_ref, lng_ref, lnb_ref, sc2_ref, sh2_ref, wrh_ref, wrl_ref,
                    br_ref, x1_ref, u2p_ref, lg_ref, acc_ref, *, nn):
    j = pl.program_id(1)
    acc_ref[j] = _dot(m_ref[...], w_ref[...])

    @pl.when(j == nn - 1)
    def _():
        mix = jnp.concatenate([acc_ref[jj] for jj in range(nn)], axis=1)
        x1 = _ln(DN_ALPHA * x_ref[...] + g1_ref[...] * mix) * lng_ref[...] + lnb_ref[...]
        x1_ref[...] = x1
        u2 = _ln(x1) * (1.0 + sc2_ref[...]) + sh2_ref[...]
        u2p_ref[...] = _pack_halves(u2)
        uh, ul = _split2(u2)
        lg_ref[...] = (_dot(uh, wrh_ref[...]) + _dot(ul, wrh_ref[...]) + _dot(uh, wrl_ref[...])
                       + br_ref[...])


def _outproj(merged, w_out, x2, gate1, ln_g, ln_b, scale2, shift2, w_router, b_router):
    s, d = x2.shape
    tm = _tile(s, 256, SUBLANES)
    tn = _tile(d, 512)
    nn = d // tn
    wrh, wrl = _split2(w_router)
    row = pl.BlockSpec((1, d), lambda i, j: (0, 0))
    wr = pl.BlockSpec((d, LANES), lambda i, j: (0, 0))
    kern = functools.partial(_outproj_kernel, nn=nn)
    return pl.pallas_call(
        kern,
        grid=(s // tm, nn),
        in_specs=[pl.BlockSpec((tm, d), lambda i, j: (i, 0)),
                  pl.BlockSpec((d, tn), lambda i, j: (0, j)),
                  pl.BlockSpec((tm, d), lambda i, j: (i, 0)),
                  row, row, row, row, row, wr, wr,
                  pl.BlockSpec((1, LANES), lambda i, j: (0, 0))],
        out_specs=[pl.BlockSpec((tm, d), lambda i, j: (i, 0)),
                   pl.BlockSpec((tm, d // 2), lambda i, j: (i, 0)),
                   pl.BlockSpec((tm, LANES), lambda i, j: (i, 0))],
        out_shape=[jax.ShapeDtypeStruct((s, d), F32),
                   jax.ShapeDtypeStruct((s, d // 2), U32),
                   jax.ShapeDtypeStruct((s, LANES), F32)],
        scratch_shapes=[pltpu.VMEM((nn, tm, tn), F32)],
        compiler_params=_cparams("parallel", "arbitrary"),
        name="outproj",
    )(merged, w_out, x2, gate1, ln_g, ln_b, scale2, shift2, wrh, wrl, b_router)


def _route_kernel(lg_ref, ei_ref, wf_ref, cnt_ref, carry_ref, *, tm):
    i = pl.program_id(0)

    @pl.when(i == 0)
    def _():
        carry_ref[...] = jnp.zeros_like(carry_ref)

    ng, epg = N_GROUPS, EXPERTS_PER_GROUP
    lg = lg_ref[...]
    lane = lax.broadcasted_iota(I32, lg.shape, 1)
    lane_f = lane.astype(F32)
    neg = -jnp.inf
    big = float(LANES)

    def first_argmax(vals, vmax):
        return jnp.min(jnp.where(vals == vmax, lane_f, big), axis=1, keepdims=True)

    gmask = lane < ng
    gl = jnp.where(gmask, lg, neg)
    gmax = jnp.max(gl, axis=1, keepdims=True)
    gidx = first_argmax(gl, gmax).astype(I32)
    gsum = jnp.sum(jnp.where(gmask, jnp.exp(lg - gmax), 0.0), axis=1, keepdims=True)
    g_w = 1.0 / gsum

    lo = ng + gidx * epg
    el = jnp.where(jnp.logical_and(lane >= lo, lane < lo + epg), lg, neg)
    t1 = jnp.max(el, axis=1, keepdims=True)
    i1 = first_argmax(el, t1).astype(I32)
    el2 = jnp.where(lane == i1, neg, el)
    t2 = jnp.max(el2, axis=1, keepdims=True)
    i2 = first_argmax(el2, t2).astype(I32)
    ex = jnp.exp(t2 - t1)
    w1 = g_w / (1.0 + ex)
    w2 = g_w * ex / (1.0 + ex)

    hit1 = lane == i1
    hit2 = lane == i2
    onehot = jnp.logical_or(hit1, hit2)
    r = lax.broadcasted_iota(I32, (tm, tm), 0)
    c = lax.broadcasted_iota(I32, (tm, tm), 1)
    before = _dot((c < r).astype(BF16), onehot.astype(BF16)) + carry_ref[...]
    rank1 = jnp.sum(jnp.where(hit1, before, 0.0), axis=1, keepdims=True).astype(I32)
    rank2 = jnp.sum(jnp.where(hit2, before, 0.0), axis=1, keepdims=True).astype(I32)
    carry_ref[...] += jnp.sum(onehot.astype(F32), axis=0, keepdims=True)
    cnt_ref[...] = carry_ref[...]

    ei = jnp.where(lane == 0, i1 - ng, jnp.where(lane == 1, i2 - ng, jnp.where(lane == 2, rank1, rank2)))
    ei_ref[...] = ei[:, :SUBLANES]
    wf_ref[...] = jnp.where(lane == 0, w1, w2)[:, :SUBLANES]


def _route(logits):
    s = logits.shape[0]
    tm = min(512, s)
    kern = functools.partial(_route_kernel, tm=tm)
    return pl.pallas_call(
        kern,
        grid=(s // tm,),
        in_specs=[pl.BlockSpec((tm, LANES), lambda i: (i, 0))],
        out_specs=[pl.BlockSpec((tm, SUBLANES), lambda i: (i, 0)),
                   pl.BlockSpec((tm, SUBLANES), lambda i: (i, 0)),
                   pl.BlockSpec((1, LANES), lambda i: (0, 0))],
        out_shape=[jax.ShapeDtypeStruct((s, SUBLANES), I32),
                   jax.ShapeDtypeStruct((s, SUBLANES), F32),
                   jax.ShapeDtypeStruct((1, LANES), F32)],
        scratch_shapes=[pltpu.VMEM((1, LANES), F32)],
        compiler_params=_cparams("arbitrary"),
        name="route",
    )(logits)


def _dispatch_kernel(p1_ref, p2_ref, src_ref, zeros_ref, dst_ref, sem, *, n_tok, chunk):
    del zeros_ref
    n_chunks = n_tok // chunk

    def row_copy(t, p, slot):
        return pltpu.make_async_copy(src_ref.at[pl.ds(t, 1)], dst_ref.at[pl.ds(p, 1)], sem.at[slot])

    def issue(ci):
        slot = ci % 2

        def body(r, carry):
            t = ci * chunk + r
            row_copy(t, p1_ref[t], slot).start()
            row_copy(t, p2_ref[t], slot).start()
            return carry

        lax.fori_loop(0, chunk, body, 0, unroll=8)

    def drain(ci):
        slot = ci % 2
        for _ in range(2):
            pltpu.make_async_copy(src_ref.at[pl.ds(0, chunk)], dst_ref.at[pl.ds(0, chunk)],
                                  sem.at[slot]).wait()

    issue(0)

    def step(ci, carry):
        issue(ci)
        drain(ci - 1)
        return carry

    lax.fori_loop(1, n_chunks, step, 0)
    drain(n_chunks - 1)


def _dispatch(pos1, pos2, u2p, n_rows):
    s, w = u2p.shape
    chunk = min(256, s)
    kern = functools.partial(_dispatch_kernel, n_tok=s, chunk=chunk)
    return pl.pallas_call(
        kern,
        grid_spec=pltpu.PrefetchScalarGridSpec(
            num_scalar_prefetch=2,
            grid=(1,),
            in_specs=[pl.BlockSpec(memory_space=pl.ANY), pl.BlockSpec(memory_space=pl.ANY)],
            out_specs=pl.BlockSpec(memory_space=pl.ANY),
            scratch_shapes=[pltpu.SemaphoreType.DMA((2,))]),
        out_shape=jax.ShapeDtypeStruct((n_rows, w), U32),
        input_output_aliases={3: 0},
        compiler_params=_cparams("arbitrary"),
        name="dispatch",
    )(pos1, pos2, u2p, jnp.zeros((n_rows, w), U32))


def _ffn_a_kernel(te_ref, na_ref, x_ref, w1_ref, w3_ref, o_ref, w1b, w3b):
    t = pl.program_id(1)

    @pl.when(t < na_ref[0])
    def _():
        changed = jnp.logical_or(t == 0, te_ref[t] != te_ref[jnp.maximum(t - 1, 0)])

        @pl.when(changed)
        def _():
            w1b[...] = w1_ref[0].astype(BF16)
            w3b[...] = w3_ref[0].astype(BF16)

        lo, hi = _unpack_halves(x_ref[...])
        lo = lo.astype(BF16)
        hi = hi.astype(BF16)
        half = lo.shape[1]
        a = _dot(lo, w1b[0:half, :]) + _dot(hi, w1b[half:2 * half, :])
        b = _dot(lo, w3b[0:half, :]) + _dot(hi, w3b[half:2 * half, :])
        o_ref[...] = (a * jax.nn.sigmoid(a) * b).astype(o_ref.dtype)

    @pl.when(t >= na_ref[0])
    def _():
        o_ref[...] = jnp.zeros_like(o_ref)


def _ffn_a(tile_expert, n_active, xs, w1, w3):
    n_rows, half = xs.shape
    d = 2 * half
    de = w1.shape[2]
    ck = _tile(de, 512)
    n_tiles = n_rows // MOE_TILE

    def row_idx(c, t, te, na):
        return (jnp.minimum(t, na[0] - 1), 0)

    def w_idx(c, t, te, na):
        return (te[jnp.minimum(t, na[0] - 1)], 0, c)

    def out_idx(c, t, te, na):
        return (t, c)

    return pl.pallas_call(
        _ffn_a_kernel,
        grid_spec=pltpu.PrefetchScalarGridSpec(
            num_scalar_prefetch=2,
            grid=(de // ck, n_tiles),
            in_specs=[pl.BlockSpec((MOE_TILE, half), row_idx),
                      pl.BlockSpec((1, d, ck), w_idx),
                      pl.BlockSpec((1, d, ck), w_idx)],
            out_specs=pl.BlockSpec((MOE_TILE, ck), out_idx),
            scratch_shapes=[pltpu.VMEM((d, ck), BF16), pltpu.VMEM((d, ck), BF16)]),
        out_shape=jax.ShapeDtypeStruct((n_rows, de), BF16),
        compiler_params=_cparams("arbitrary", "arbitrary"),
        name="ffn_a",
    )(tile_expert, n_active, xs, w1, w3)


def _ffn_b_kernel(te_ref, na_ref, h_ref, w2_ref, o_ref, w2b):
    t = pl.program_id(0)

    @pl.when(t < na_ref[0])
    def _():
        changed = jnp.logical_or(t == 0, te_ref[t] != te_ref[jnp.maximum(t - 1, 0)])

        @pl.when(changed)
        def _():
            w2b[...] = w2_ref[0].astype(BF16)

        o_ref[...] = _pack_halves(_dot(h_ref[...], w2b[...]))

    @pl.when(t >= na_ref[0])
    def _():
        o_ref[...] = jnp.zeros_like(o_ref)


def _ffn_b(tile_expert, n_active, hid, w2):
    n_rows, de = hid.shape
    d = w2.shape[2]
    n_tiles = n_rows // MOE_TILE

    def row_idx(t, te, na):
        return (jnp.minimum(t, na[0] - 1), 0)

    def w_idx(t, te, na):
        return (te[jnp.minimum(t, na[0] - 1)], 0, 0)

    return pl.pallas_call(
        _ffn_b_kernel,
        grid_spec=pltpu.PrefetchScalarGridSpec(
            num_scalar_prefetch=2,
            grid=(n_tiles,),
            in_specs=[pl.BlockSpec((MOE_TILE, de), row_idx),
                      pl.BlockSpec((1, de, d), w_idx)],
            out_specs=pl.BlockSpec((MOE_TILE, d // 2), lambda t, te, na: (t, 0)),
            scratch_shapes=[pltpu.VMEM((de, d), BF16)]),
        out_shape=jax.ShapeDtypeStruct((n_rows, d // 2), U32),
        compiler_params=_cparams("arbitrary"),
        name="ffn_b",
    )(tile_expert, n_active, hid, w2)


def _combine_kernel(p1_ref, p2_ref, y_ref, x1_ref, wf_ref, g2_ref, lng_ref, lnb_ref, o_ref, ybuf, sem, *, tm):
    i = pl.program_id(0)
    n = pl.num_programs(0)

    def issue(tile, slot):
        def body(r, carry):
            t = tile * tm + r
            pltpu.make_async_copy(y_ref.at[pl.ds(p1_ref[t], 1)], ybuf.at[slot, 0, pl.ds(r, 1)],
                                  sem.at[slot]).start()
            pltpu.make_async_copy(y_ref.at[pl.ds(p2_ref[t], 1)], ybuf.at[slot, 1, pl.ds(r, 1)],
                                  sem.at[slot]).start()
            return carry

        lax.fori_loop(0, tm, body, 0, unroll=8)

    @pl.when(i == 0)
    def _():
        issue(0, 0)

    @pl.when(i + 1 < n)
    def _():
        issue(i + 1, (i + 1) % 2)

    slot = i % 2
    for k in range(2):
        pltpu.make_async_copy(y_ref.at[pl.ds(0, tm)], ybuf.at[slot, k], sem.at[slot]).wait()

    wf = wf_ref[...]
    w1 = wf[:, 0:1]
    w2 = wf[:, 1:2]
    lo1, hi1 = _unpack_halves(ybuf[slot, 0])
    lo2, hi2 = _unpack_halves(ybuf[slot, 1])
    ffn = jnp.concatenate([w1 * lo1 + w2 * lo2, w1 * hi1 + w2 * hi2], axis=1)
    o_ref[...] = _ln(DN_ALPHA * x1_ref[...] + g2_ref[...] * ffn) * lng_ref[...] + lnb_ref[...]


def _combine(pos1, pos2, y, x1, wf, gate2, ln_g, ln_b):
    s, d = x1.shape
    tm = min(256, s)
    row = pl.BlockSpec((1, d), lambda i, p1, p2: (0, 0))
    kern = functools.partial(_combine_kernel, tm=tm)
    return pl.pallas_call(
        kern,
        grid_spec=pltpu.PrefetchScalarGridSpec(
            num_scalar_prefetch=2,
            grid=(s // tm,),
            in_specs=[pl.BlockSpec(memory_space=pl.ANY),
                      pl.BlockSpec((tm, d), lambda i, p1, p2: (i, 0)),
                      pl.BlockSpec((tm, SUBLANES), lambda i, p1, p2: (i, 0)),
                      row, row, row],
            out_specs=pl.BlockSpec((tm, d), lambda i, p1, p2: (i, 0)),
            scratch_shapes=[pltpu.VMEM((2, 2, tm, d // 2), U32), pltpu.SemaphoreType.DMA((2,))]),
        out_shape=jax.ShapeDtypeStruct((s, d), F32),
        compiler_params=_cparams("arbitrary"),
        name="combine",
    )(pos1, pos2, y, x1, wf, gate2, ln_g, ln_b)


def _layer(x2, c, w_ada, b_ada, w_in, b_if, conv_w, conv_b, head_norm_g, w_branch_sb, w_branch_ml, w_out,
           ln1_g, ln1_b, w_rg, b_rg, w_re, b_re, w1, w3, w2, ln2_g, ln2_b):
    s, d = x2.shape
    sbw = SB_HEADS * SB_HEAD_DIM
    mlw = ML_HEADS * ML_V_DIM
    qkw2 = 2 * ML_HEADS * ML_QK_DIM
    n_exp = N_GROUPS * EXPERTS_PER_GROUP

    mod = _mod(c.reshape(d, 1), w_ada, b_ada.reshape(1, -1))
    shift1, scale1, gate1, shift2, scale2, gate2 = [mod[:, i * d:(i + 1) * d] for i in range(N_MOD)]

    o_q, o_k, o_v = 0, sbw, 2 * sbw
    o_qk = 3 * sbw
    o_vml = o_qk + qkw2
    o_oml = o_vml + mlw
    o_if = o_oml + mlw
    o_gsb = o_if + 2 * ML_HEADS
    o_gml = o_gsb + d
    cols = lambda o, n: w_in[:, o:o + n]
    w_main = jnp.concatenate([cols(o_vml, mlw), cols(o_oml, mlw), cols(o_q, sbw), cols(o_k, sbw),
                              cols(o_v, sbw), cols(o_gsb, d), cols(o_gml, d)], axis=1).astype(BF16)
    p_q, p_k, p_v = 2 * mlw, 2 * mlw + sbw, 2 * mlw + 2 * sbw
    p_gsb = 2 * mlw + 3 * sbw
    p_gml = p_gsb + d

    u1, if_col, if_row = _ln1(x2, scale1, shift1, cols(o_if, 2 * ML_HEADS))
    proj = _matmul(u1, w_main, BF16, "inproj")
    qk_pre = _matmul(u1, cols(o_qk, qkw2).astype(BF16), F32, "inproj_qk")

    dh = SB_HEAD_DIM
    y_sb = _attention(proj, p_q // dh, p_k // dh, p_v // dh)
    h_ml = _mlstm(qk_pre, proj, 0, 1, if_col, if_row, conv_w, conv_b, b_if, head_norm_g)
    merged = _merge(y_sb, h_ml, w_branch_sb.astype(BF16), w_branch_ml.astype(BF16), proj, p_gsb, p_gml)

    w_router = jnp.zeros((d, LANES), F32).at[:, :N_GROUPS].set(w_rg).at[:, N_GROUPS:N_GROUPS + n_exp].set(w_re)
    b_router = jnp.zeros((1, LANES), F32).at[0, :N_GROUPS].set(b_rg).at[0, N_GROUPS:N_GROUPS + n_exp].set(b_re)
    x1, u2p, logits = _outproj(merged, w_out.astype(BF16), x2, gate1, ln1_g.reshape(1, d), ln1_b.reshape(1, d),
                               scale2, shift2, w_router, b_router)

    ei, wf, cnt = _route(logits)
    counts = cnt[0, N_GROUPS:N_GROUPS + n_exp].astype(I32)
    padded = ((counts + MOE_TILE - 1) // MOE_TILE) * MOE_TILE
    pend = jnp.cumsum(padded)
    pstart = pend - padded
    pos1 = pstart[ei[:, 0]] + ei[:, 2]
    pos2 = pstart[ei[:, 1]] + ei[:, 3]
    n_tiles = -(-(s * TOP_K_IN_GROUP) // MOE_TILE) + n_exp
    tile_expert = jnp.clip(jnp.searchsorted(pend, jnp.arange(n_tiles, dtype=I32) * MOE_TILE, side="right"),
                           0, n_exp - 1).astype(I32)
    n_active = (pend[-1:] // MOE_TILE).astype(I32)

    xs = _dispatch(pos1, pos2, u2p, n_tiles * MOE_TILE)
    hid = _ffn_a(tile_expert, n_active, xs, w1, w3)
    y = _ffn_b(tile_expert, n_active, hid, w2)
    return _combine(pos1, pos2, y, x1, wf, gate2, ln2_g.reshape(1, d), ln2_b.reshape(1, d))


def kernel(x, c, w_ada, b_ada, w_in, b_if, conv_w, conv_b, head_norm_g, w_branch_sb, w_branch_ml, w_out,
           ln1_g, ln1_b, w_rg, b_rg, w_re, b_re, w1, w3, w2, ln2_g, ln2_b):
    bsz, seq, d = x.shape
    assert bsz == 1 and w_ada.shape[0] == DEPTH
    x2 = x.reshape(seq, d)
    for l in range(DEPTH):
        x2 = _layer(x2, c, w_ada[l], b_ada[l], w_in[l], b_if[l], conv_w[l], conv_b[l], head_norm_g[l],
                    w_branch_sb[l], w_branch_ml[l], w_out[l], ln1_g[l], ln1_b[l], w_rg[l], b_rg[l],
                    w_re[l], b_re[l], w1[l], w3[l], w2[l], ln2_g[l], ln2_b[l])
    return x2.reshape(bsz, seq, d)
```

```python
import functools

import jax
import jax.numpy as jnp
from jax import lax
from jax.experimental import pallas as pl
from jax.experimental.pallas import tpu as pltpu

F32 = jnp.float32
BF16 = jnp.bfloat16
I32 = jnp.int32

DEPTH = 1
SB_HEADS = 16
SB_HEAD_DIM = 128
ML_HEADS = 8
ML_V_DIM = 256
ML_QK_DIM = 128
CONV_WIDTH = 4
N_GROUPS = 4
EXPERTS_PER_GROUP = 8
TOP_K_IN_GROUP = 2
N_MOD = 6
DN_ALPHA = (2.0 * DEPTH) ** 0.25
LN_EPS = 1e-5

LANES = 128
SUBLANES = 8
VMEM_LIMIT_BYTES = 60 * 1024 * 1024

EXP_UNDERFLOW = -104.0
OUT_OF_RANGE = -1e30

MOE_TILE = 256
ATTN_HEADS_PER_STEP = 2
ATTN_WALK_UNROLL = 2


def _tile(n, pref, align=LANES):
    if n <= pref:
        return n
    t = (pref // align) * align
    while n % t:
        t -= align
    return t


def _cparams(*sem):
    return pltpu.CompilerParams(dimension_semantics=tuple(sem), vmem_limit_bytes=VMEM_LIMIT_BYTES)


def _ln(x):
    mu = jnp.mean(x, axis=-1, keepdims=True)
    xc = x - mu
    var = jnp.mean(xc * xc, axis=-1, keepdims=True)
    return xc * lax.rsqrt(var + LN_EPS)


def _split2(x):
    hi = x.astype(BF16)
    lo = (x - hi.astype(F32)).astype(BF16)
    return hi, lo


def _split3(x):
    hi = x.astype(BF16)
    r = x - hi.astype(F32)
    mid = r.astype(BF16)
    lo = (r - mid.astype(F32)).astype(BF16)
    return hi, mid, lo


def _dot(a, b):
    return jnp.dot(a, b, preferred_element_type=F32)


def _dot_nt(a, b):
    return lax.dot_general(a, b, (((1,), (1,)), ((), ())), preferred_element_type=F32)


def _dot_tn(a, b):
    return lax.dot_general(a, b, (((0,), (0,)), ((), ())), preferred_element_type=F32)


def _softplus(z):
    return jnp.maximum(z, 0.0) + jnp.log(1.0 + jnp.exp(-jnp.abs(z)))


def _mod_kernel(c_ref, w_ref, b_ref, o_ref):
    @pl.when(pl.program_id(1) == 0)
    def _():
        o_ref[...] = b_ref[...]

    c = c_ref[...]
    s = c * jax.nn.sigmoid(c)
    o_ref[...] += jnp.sum(w_ref[...] * s, axis=0, keepdims=True)


def _mod(c_col, w_ada, b_ada):
    d, n = w_ada.shape
    tk = _tile(d, 512, SUBLANES)
    tn = _tile(n, 2048)
    return pl.pallas_call(
        _mod_kernel,
        grid=(n // tn, d // tk),
        in_specs=[pl.BlockSpec((tk, 1), lambda j, k: (k, 0)),
                  pl.BlockSpec((tk, tn), lambda j, k: (k, j)),
                  pl.BlockSpec((1, tn), lambda j, k: (0, j))],
        out_specs=pl.BlockSpec((1, tn), lambda j, k: (0, j)),
        out_shape=jax.ShapeDtypeStruct((1, n), F32),
        compiler_params=_cparams("parallel", "arbitrary"),
        name="mod",
    )(c_col, w_ada, b_ada)


def _ln1_kernel(x_ref, sc_ref, sh_ref, wh_ref, wl_ref, wth_ref, wtl_ref, u_ref, if_ref, ift_ref):
    u = _ln(x_ref[...]) * (1.0 + sc_ref[...]) + sh_ref[...]
    u_ref[...] = u.astype(BF16)
    uh, ul = _split2(u)
    if_ref[...] = _dot(uh, wh_ref[...]) + _dot(ul, wh_ref[...]) + _dot(uh, wl_ref[...])
    ift_ref[...] = (_dot_nt(wth_ref[...], uh) + _dot_nt(wth_ref[...], ul) + _dot_nt(wtl_ref[...], uh))


def _ln1(x2, scale, shift, w_if_t):
    s, d = x2.shape
    tm = _tile(s, 256, SUBLANES)
    wpad_t = jnp.zeros((LANES, d), F32).at[:w_if_t.shape[0], :].set(w_if_t)
    wth, wtl = _split2(wpad_t)
    wh, wl = wth.T, wtl.T
    row = pl.BlockSpec((1, d), lambda i: (0, 0))
    wspec = pl.BlockSpec((d, LANES), lambda i: (0, 0))
    wtspec = pl.BlockSpec((LANES, d), lambda i: (0, 0))
    return pl.pallas_call(
        _ln1_kernel,
        grid=(s // tm,),
        in_specs=[pl.BlockSpec((tm, d), lambda i: (i, 0)), row, row, wspec, wspec, wtspec, wtspec],
        out_specs=[pl.BlockSpec((tm, d), lambda i: (i, 0)),
                   pl.BlockSpec((tm, LANES), lambda i: (i, 0)),
                   pl.BlockSpec((LANES, tm), lambda i: (0, i))],
        out_shape=[jax.ShapeDtypeStruct((s, d), BF16),
                   jax.ShapeDtypeStruct((s, LANES), F32),
                   jax.ShapeDtypeStruct((LANES, s), F32)],
        compiler_params=_cparams("parallel"),
        name="ln1",
    )(x2, scale, shift, wh, wl, wth, wtl)


def _mm_kernel(a_ref, b_ref, o_ref):
    o_ref[...] = _dot(a_ref[...], b_ref[...]).astype(o_ref.dtype)


def _matmul(a, b, out_dtype, name, col0=0, ncols=None):
    m, k = a.shape
    n = b.shape[1] if ncols is None else ncols
    tm = _tile(m, 1024, SUBLANES)
    tn = _tile(n, 1024)
    j0 = col0 // tn
    assert j0 * tn == col0
    return pl.pallas_call(
        _mm_kernel,
        grid=(m // tm, n // tn),
        in_specs=[pl.BlockSpec((tm, k), lambda i, j: (i, 0)),
                  pl.BlockSpec((k, tn), lambda i, j: (0, j0 + j))],
        out_specs=pl.BlockSpec((tm, tn), lambda i, j: (i, j)),
        out_shape=jax.ShapeDtypeStruct((m, n), out_dtype),
        compiler_params=_cparams("parallel", "arbitrary"),
        name=name,
    )(a, b)


def _mm_wt_kernel(a_ref, w_ref, o_ref, wb):
    @pl.when(pl.program_id(1) == 0)
    def _():
        wb[...] = w_ref[...].astype(BF16)

    o_ref[...] = _dot_nt(a_ref[...], wb[...]).astype(o_ref.dtype)


def _matmul_wt(a, wt, row0, nrows, out_dtype, name):
    m, k = a.shape
    tm = _tile(m, 1024, SUBLANES)
    tn = _tile(nrows, 512)
    assert row0 % SUBLANES == 0
    return pl.pallas_call(
        _mm_wt_kernel,
        grid=(nrows // tn, m // tm),
        in_specs=[pl.BlockSpec((tm, k), lambda j, i: (i, 0)),
                  pl.BlockSpec((pl.Element(tn), pl.Element(k)), lambda j, i: (pl.multiple_of(row0 + j * tn, SUBLANES), 0))],
        out_specs=pl.BlockSpec((tm, tn), lambda j, i: (i, j)),
        out_shape=jax.ShapeDtypeStruct((m, nrows), out_dtype),
        scratch_shapes=[pltpu.VMEM((tn, k), BF16)],
        compiler_params=_cparams("arbitrary", "arbitrary"),
        name=name,
    )(a, wt)


def _attn_kernel(q_ref, k_ref, v_ref, o_ref, *, tq, scale):
    qb = pl.program_id(1)
    dh = SB_HEAD_DIM
    nh = q_ref.shape[1] // dh
    row = lax.broadcasted_iota(I32, (tq, tq), 0)
    col = lax.broadcasted_iota(I32, (tq, tq), 1)
    strict = col < row
    later = (row > col).astype(BF16)
    qs = [q_ref[:, i * dh:(i + 1) * dh] for i in range(nh)]

    def tile(i, kb, c, diagonal):
        start = pl.multiple_of(kb * tq, tq)
        k = k_ref[pl.ds(start, tq), i * dh:(i + 1) * dh]
        v = v_ref[pl.ds(start, tq), i * dh:(i + 1) * dh]
        z = _dot_nt(qs[i], k) * scale
        log_keep = -_softplus(z)
        if diagonal:
            log_keep = jnp.where(strict, log_keep, 0.0)
        hi, lo = _split2(log_keep)
        between = _dot(hi, later) + _dot(lo, later) + c
        w = jnp.exp(z + log_keep + between)
        if diagonal:
            w = jnp.where(strict, w, 0.0)
        pv = _dot(w.astype(BF16), v)
        return pv, c + jnp.sum(log_keep, axis=1, keepdims=True)

    zero_c = jnp.zeros((tq, 1), F32)
    first = [tile(i, qb, zero_c, True) for i in range(nh)]
    accs = tuple(p for p, _ in first)
    cs = tuple(c for _, c in first)

    def cmax_of(cs):
        m = jnp.max(cs[0])
        for c in cs[1:]:
            m = jnp.maximum(m, jnp.max(c))
        return m

    def cond(st):
        j, _, _, cmax = st
        return jnp.logical_and(j <= qb, cmax > EXP_UNDERFLOW)

    def body(st):
        j, accs, cs, _ = st
        accs, cs = list(accs), list(cs)
        for u in range(ATTN_WALK_UNROLL):
            kb = qb - j - u
            in_range = kb >= 0
            kb = jnp.maximum(kb, 0)
            for i in range(nh):
                c_in = cs[i] if u == 0 else jnp.where(in_range, cs[i], OUT_OF_RANGE)
                pv, c_out = tile(i, kb, c_in, False)
                accs[i] = accs[i] + pv
                cs[i] = c_out
        return j + ATTN_WALK_UNROLL, tuple(accs), tuple(cs), cmax_of(cs)

    _, accs, _, _ = lax.while_loop(cond, body, (jnp.int32(1), accs, cs, cmax_of(cs)))
    for i in range(nh):
        o_ref[:, i * dh:(i + 1) * dh] = accs[i].astype(o_ref.dtype)


def _attention(qkv):
    s = qkv.shape[0]
    tq = _tile(s, 256, SUBLANES)
    dh = SB_HEAD_DIM
    hps = ATTN_HEADS_PER_STEP
    groups = SB_HEADS // hps
    w = hps * dh
    kern = functools.partial(_attn_kernel, tq=tq, scale=dh ** -0.5)
    return pl.pallas_call(
        kern,
        grid=(groups, s // tq),
        in_specs=[pl.BlockSpec((tq, w), lambda g, i: (i, g)),
                  pl.BlockSpec((s, w), lambda g, i: (0, groups + g)),
                  pl.BlockSpec((s, w), lambda g, i: (0, 2 * groups + g))],
        out_specs=pl.BlockSpec((tq, w), lambda g, i: (i, g)),
        out_shape=jax.ShapeDtypeStruct((s, SB_HEADS * dh), BF16),
        compiler_params=_cparams("parallel", "arbitrary"),
        name="attn",
    )(qkv, qkv, qkv)


def _mlstm_kernel(qk_ref, v_ref, o_ref, if_ref, ift_ref, cw_ref, cb_ref, brow_ref, bcol_ref, hg_ref,
                  out_ref, xbuf, ct_ref, nt_ref, m_ref, *, tr):
    heads, dk, dv = ML_HEADS, ML_QK_DIM, ML_V_DIM
    qkw = heads * dk
    halo = SUBLANES

    @pl.when(pl.program_id(0) == 0)
    def _():
        xbuf[0:halo, :] = jnp.zeros((halo, 2 * qkw), F32)
        ct_ref[...] = jnp.zeros_like(ct_ref)
        nt_ref[...] = jnp.zeros_like(nt_ref)
        m_ref[...] = jnp.zeros_like(m_ref)

    xbuf[halo:halo + tr, :] = qk_ref[...]
    y = cb_ref[...]
    for i in range(CONV_WIDTH):
        y = y + cw_ref[i:i + 1, :] * xbuf[pl.ds(halo - (CONV_WIDTH - 1) + i, tr), :]
    xbuf[0:halo, :] = xbuf[tr:tr + halo, :]
    qk = y * jax.nn.sigmoid(y)

    g_col = if_ref[...] + brow_ref[...]
    g_row = ift_ref[...] + bcol_ref[...]
    lf_col = -_softplus(-g_col)
    lf_row = -_softplus(-g_row)
    r = lax.broadcasted_iota(I32, (tr, tr), 0)
    s = lax.broadcasted_iota(I32, (tr, tr), 1)
    causal = s <= r
    incl_col = causal.astype(BF16)
    incl_row = (r <= s).astype(BF16)
    c1, c2, c3 = _split3(lf_col)
    b_col_all = _dot(incl_col, c1) + _dot(incl_col, c2) + _dot(incl_col, c3)
    r1, r2, r3 = _split3(lf_row)
    b_row_all = _dot(r1, incl_row) + _dot(r2, incl_row) + _dot(r3, incl_row)

    for h in range(heads):
        q = qk[:, h * dk:(h + 1) * dk].astype(BF16)
        k = (qk[:, qkw + h * dk:qkw + (h + 1) * dk] * (dk ** -0.5)).astype(BF16)
        v = v_ref[:, h * dv:(h + 1) * dv]
        ig_col = g_col[:, h:h + 1]
        b_col = b_col_all[:, heads + h:heads + h + 1]
        ig_row = g_row[h:h + 1, :]
        b_row = b_row_all[heads + h:heads + h + 1, :]
        g_tot = b_row[:, tr - 1:tr]
        m_prev = m_ref[h][:, 0:1]

        dmat = jnp.where(causal, b_col - b_row + ig_row, -jnp.inf)
        m_inter = b_col + m_prev
        m_row = jnp.maximum(m_inter, jnp.max(dmat, axis=1, keepdims=True))
        p = jnp.exp(dmat - m_row) * _dot_nt(q, k)
        inter_scale = jnp.exp(m_inter - m_row)
        inter_num = _dot(q, ct_ref[h].astype(BF16))
        inter_den = _dot(q, nt_ref[h].astype(BF16))[:, 0:1]
        num = _dot(p.astype(BF16), v) + inter_scale * inter_num
        den = jnp.sum(p, axis=1, keepdims=True) + inter_scale * inter_den
        hh = num / jnp.maximum(jnp.abs(den), jnp.exp(-m_row))

        a_row = g_tot - b_row + ig_row
        a_col = g_tot - b_col + ig_col
        m_new = jnp.maximum(g_tot + m_prev, jnp.max(a_row, axis=1, keepdims=True))
        decay = jnp.exp(g_tot + m_prev - m_new)
        w_col = jnp.exp(a_col - m_new)
        ct_ref[h] = decay * ct_ref[h] + _dot_tn(k, (w_col * v.astype(F32)).astype(BF16))
        nt_ref[h] = decay * nt_ref[h] + _dot_tn(k, jnp.broadcast_to(w_col, (tr, LANES)).astype(BF16))
        m_ref[h] = jnp.broadcast_to(m_new, (1, LANES))

        hn = _ln(hh) * hg_ref[:, h * dv:(h + 1) * dv]
        gate = jax.nn.sigmoid(o_ref[:, h * dv:(h + 1) * dv].astype(F32))
        out_ref[:, h * dv:(h + 1) * dv] = (gate * hn).astype(out_ref.dtype)


def _mlstm(qk_pre, vo, if_col, if_row, conv_w, conv_b, b_if, head_norm_g):
    s = qk_pre.shape[0]
    tr = _tile(s, 256, SUBLANES)
    heads, dk, dv = ML_HEADS, ML_QK_DIM, ML_V_DIM
    mlw = heads * dv
    qkw2 = 2 * heads * dk
    brow = jnp.zeros((1, LANES), F32).at[0, :2 * heads].set(b_if)
    full = lambda shape: pl.BlockSpec(shape, lambda i: (0,) * len(shape))
    kern = functools.partial(_mlstm_kernel, tr=tr)
    return pl.pallas_call(
        kern,
        grid=(s // tr,),
        in_specs=[pl.BlockSpec((tr, qkw2), lambda i: (i, 0)),
                  pl.BlockSpec((tr, mlw), lambda i: (i, 0)),
                  pl.BlockSpec((tr, mlw), lambda i: (i, 1)),
                  pl.BlockSpec((tr, LANES), lambda i: (i, 0)),
                  pl.BlockSpec((LANES, tr), lambda i: (0, i)),
                  full((CONV_WIDTH, qkw2)), full((1, qkw2)), full((1, LANES)), full((LANES, 1)),
                  full((1, mlw))],
        out_specs=pl.BlockSpec((tr, mlw), lambda i: (i, 0)),
        out_shape=jax.ShapeDtypeStruct((s, mlw), BF16),
        scratch_shapes=[pltpu.VMEM((tr + 2 * SUBLANES, qkw2), F32),
                        pltpu.VMEM((heads, dk, dv), F32),
                        pltpu.VMEM((heads, dk, LANES), F32),
                        pltpu.VMEM((heads, 1, LANES), F32)],
        compiler_params=_cparams("arbitrary"),
        name="mlstm",
    )(qk_pre, vo, vo, if_col, if_row, conv_w, conv_b.reshape(1, -1), brow, brow.reshape(LANES, 1),
      head_norm_g.reshape(1, -1))


def _merge_kernel(y_ref, h_ref, wsb_ref, wml_ref, gsb_ref, gml_ref, o_ref):
    a = _dot(y_ref[...], wsb_ref[...])
    b = _dot(h_ref[...], wml_ref[...])
    o = jax.nn.sigmoid(gsb_ref[...].astype(F32)) * a + jax.nn.sigmoid(gml_ref[...].astype(F32)) * b
    o_ref[...] = o.astype(o_ref.dtype)


def _merge(y_sb, h_ml, w_sb, w_ml, gates):
    s = y_sb.shape[0]
    d = w_sb.shape[1]
    tm = _tile(s, 512, SUBLANES)
    tn = _tile(d, 1024)
    nj = d // tn
    return pl.pallas_call(
        _merge_kernel,
        grid=(s // tm, nj),
        in_specs=[pl.BlockSpec((tm, y_sb.shape[1]), lambda i, j: (i, 0)),
                  pl.BlockSpec((tm, h_ml.shape[1]), lambda i, j: (i, 0)),
                  pl.BlockSpec((w_sb.shape[0], tn), lambda i, j: (0, j)),
                  pl.BlockSpec((w_ml.shape[0], tn), lambda i, j: (0, j)),
                  pl.BlockSpec((tm, tn), lambda i, j: (i, j)),
                  pl.BlockSpec((tm, tn), lambda i, j: (i, nj + j))],
        out_specs=pl.BlockSpec((tm, tn), lambda i, j: (i, j)),
        out_shape=jax.ShapeDtypeStruct((s, d), BF16),
        compiler_params=_cparams("parallel", "arbitrary"),
        name="merge",
    )(y_sb, h_ml, w_sb, w_ml, gates, gates)


def _mid_kernel(mix_ref, x_ref, g1_ref, lng_ref, lnb_ref, sc2_ref, sh2_ref, wrh_ref, wrl_ref, br_ref,
                x1_ref, u2_ref, lg_ref):
    x1 = _ln(DN_ALPHA * x_ref[...] + g1_ref[...] * mix_ref[...]) * lng_ref[...] + lnb_ref[...]
    x1_ref[...] = x1
    u2 = _ln(x1) * (1.0 + sc2_ref[...]) + sh2_ref[...]
    u2_ref[...] = u2.astype(BF16)
    uh, ul = _split2(u2)
    lg_ref[...] = (_dot(uh, wrh_ref[...]) + _dot(ul, wrh_ref[...]) + _dot(uh, wrl_ref[...]) + br_ref[...])


def _mid(mix, x2, gate1, ln_g, ln_b, scale2, shift2, w_router, b_router):
    s, d = x2.shape
    tm = _tile(s, 256, SUBLANES)
    wrh, wrl = _split2(w_router)
    row = pl.BlockSpec((1, d), lambda i: (0, 0))
    wr = pl.BlockSpec((d, LANES), lambda i: (0, 0))
    tile = pl.BlockSpec((tm, d), lambda i: (i, 0))
    return pl.pallas_call(
        _mid_kernel,
        grid=(s // tm,),
        in_specs=[tile, tile, row, row, row, row, row, wr, wr, pl.BlockSpec((1, LANES), lambda i: (0, 0))],
        out_specs=[tile, tile, pl.BlockSpec((tm, LANES), lambda i: (i, 0))],
        out_shape=[jax.ShapeDtypeStruct((s, d), F32),
                   jax.ShapeDtypeStruct((s, d), BF16),
                   jax.ShapeDtypeStruct((s, LANES), F32)],
        compiler_params=_cparams("parallel"),
        name="mid",
    )(mix, x2, gate1, ln_g, ln_b, scale2, shift2, wrh, wrl, b_router)


def _route_kernel(lg_ref, pos_ref, wf_ref, cnt_ref, carry_ref, pstart_ref, *, tm):
    phase = pl.program_id(0)
    i = pl.program_id(1)
    ng, epg = N_GROUPS, EXPERTS_PER_GROUP

    @pl.when(jnp.logical_and(phase == 0, i == 0))
    def _():
        carry_ref[...] = jnp.zeros_like(carry_ref)

    @pl.when(jnp.logical_and(phase == 1, i == 0))
    def _():
        counts = carry_ref[...]
        cnt_ref[...] = counts
        n_tiles = jnp.ceil(counts * (1.0 / MOE_TILE))
        a = lax.broadcasted_iota(I32, (LANES, LANES), 0)
        b = lax.broadcasted_iota(I32, (LANES, LANES), 1)
        tiles_before = _dot(jnp.broadcast_to(n_tiles, (SUBLANES, LANES)).astype(BF16), (a < b).astype(BF16))
        pstart_ref[...] = tiles_before[0:1, :] * float(MOE_TILE)
        carry_ref[...] = jnp.zeros_like(carry_ref)

    lg = lg_ref[...]
    lane = lax.broadcasted_iota(I32, lg.shape, 1)
    lane_f = lane.astype(F32)
    neg = -jnp.inf
    big = float(LANES)

    def first_argmax(vals, vmax):
        return jnp.min(jnp.where(vals == vmax, lane_f, big), axis=1, keepdims=True)

    gmask = lane < ng
    gl = jnp.where(gmask, lg, neg)
    gmax = jnp.max(gl, axis=1, keepdims=True)
    gidx = first_argmax(gl, gmax).astype(I32)
    lo = ng + gidx * epg
    el = jnp.where(jnp.logical_and(lane >= lo, lane < lo + epg), lg, neg)
    t1 = jnp.max(el, axis=1, keepdims=True)
    i1 = first_argmax(el, t1).astype(I32)
    el2 = jnp.where(lane == i1, neg, el)
    t2 = jnp.max(el2, axis=1, keepdims=True)
    i2 = first_argmax(el2, t2).astype(I32)
    hit1 = lane == i1
    hit2 = lane == i2
    onehot = jnp.logical_or(hit1, hit2)

    @pl.when(phase == 1)
    def _():
        gsum = jnp.sum(jnp.where(gmask, jnp.exp(lg - gmax), 0.0), axis=1, keepdims=True)
        g_w = 1.0 / gsum
        ex = jnp.exp(t2 - t1)
        w1 = g_w / (1.0 + ex)
        w2 = g_w * ex / (1.0 + ex)
        r = lax.broadcasted_iota(I32, (tm, tm), 0)
        c = lax.broadcasted_iota(I32, (tm, tm), 1)
        row_of = (_dot((c < r).astype(BF16), onehot.astype(BF16)) + carry_ref[...] + pstart_ref[...])
        pos1 = jnp.sum(jnp.where(hit1, row_of, 0.0), axis=1, keepdims=True).astype(I32)
        pos2 = jnp.sum(jnp.where(hit2, row_of, 0.0), axis=1, keepdims=True).astype(I32)
        pos_ref[...] = jnp.where(lane == 0, pos1, pos2)[:, :SUBLANES]
        wf_ref[...] = jnp.where(lane == 0, w1, w2)[:, :SUBLANES]

    carry_ref[...] += jnp.sum(onehot.astype(F32), axis=0, keepdims=True)


def _route(logits):
    s = logits.shape[0]
    tm = _tile(s, 512, SUBLANES)
    kern = functools.partial(_route_kernel, tm=tm)
    return pl.pallas_call(
        kern,
        grid=(2, s // tm),
        in_specs=[pl.BlockSpec((tm, LANES), lambda p, i: (i, 0))],
        out_specs=[pl.BlockSpec((tm, SUBLANES), lambda p, i: (i * p, 0)),
                   pl.BlockSpec((tm, SUBLANES), lambda p, i: (i * p, 0)),
                   pl.BlockSpec((1, LANES), lambda p, i: (0, 0))],
        out_shape=[jax.ShapeDtypeStruct((s, SUBLANES), I32),
                   jax.ShapeDtypeStruct((s, SUBLANES), F32),
                   jax.ShapeDtypeStruct((1, LANES), F32)],
        scratch_shapes=[pltpu.VMEM((1, LANES), F32), pltpu.VMEM((1, LANES), F32)],
        compiler_params=_cparams("arbitrary", "arbitrary"),
        name="route",
    )(logits)


def _dispatch_kernel(p1_ref, p2_ref, zs_ref, x_ref, dst_ref, stage, zbuf, sem, zsem, *, tm, n_zero):
    i = pl.program_id(0)
    n = pl.num_programs(0)
    slot = i % 2

    def zero_copy(e):
        start = pl.multiple_of(zs_ref[e], MOE_TILE)
        return pltpu.make_async_copy(zbuf, dst_ref.at[pl.ds(start, MOE_TILE)], zsem)

    def drain(s):
        for _ in range(TOP_K_IN_GROUP):
            pltpu.make_async_copy(stage.at[s], dst_ref.at[pl.ds(0, tm)], sem.at[s]).wait()

    @pl.when(i == 0)
    def _():
        zbuf[...] = jnp.zeros_like(zbuf)
        for e in range(n_zero):
            pl.when(zs_ref[e] >= 0)(lambda e=e: zero_copy(e).start())
        for e in range(n_zero):
            pl.when(zs_ref[e] >= 0)(lambda e=e: zero_copy(e).wait())

    @pl.when(i >= 2)
    def _():
        drain(slot)

    stage[slot] = x_ref[...].astype(F32)

    def body(r, carry):
        t = i * tm + r
        src = stage.at[slot, pl.ds(r, 1)]
        pltpu.make_async_copy(src, dst_ref.at[pl.ds(p1_ref[t], 1)], sem.at[slot]).start()
        pltpu.make_async_copy(src, dst_ref.at[pl.ds(p2_ref[t], 1)], sem.at[slot]).start()
        return carry

    lax.fori_loop(0, tm, body, 0, unroll=8)

    @pl.when(i == n - 1)
    def _():
        drain(slot)

        @pl.when(n >= 2)
        def _():
            drain(1 - slot)


def _dispatch(pos1, pos2, zero_start, u2p, n_rows):
    s, w = u2p.shape
    tm = _tile(s, 256, SUBLANES)
    kern = functools.partial(_dispatch_kernel, tm=tm, n_zero=zero_start.shape[0])
    return pl.pallas_call(
        kern,
        grid_spec=pltpu.PrefetchScalarGridSpec(
            num_scalar_prefetch=3,
            grid=(s // tm,),
            in_specs=[pl.BlockSpec((tm, w), lambda i, p1, p2, zs: (i, 0))],
            out_specs=pl.BlockSpec(memory_space=pl.ANY),
            scratch_shapes=[pltpu.VMEM((2, tm, w), F32), pltpu.VMEM((MOE_TILE, w), F32),
                            pltpu.SemaphoreType.DMA((2,)), pltpu.SemaphoreType.DMA]),
        out_shape=jax.ShapeDtypeStruct((n_rows, w), F32),
        compiler_params=_cparams("arbitrary"),
        name="dispatch",
    )(pos1, pos2, zero_start, u2p)


def _ffn_a_kernel(te_ref, na_ref, x_ref, w1_ref, w3_ref, o_ref, w1b, w3b):
    t = pl.program_id(1)

    @pl.when(t < na_ref[0])
    def _():
        changed = jnp.logical_or(t == 0, te_ref[t] != te_ref[jnp.maximum(t - 1, 0)])

        @pl.when(changed)
        def _():
            w1b[...] = w1_ref[0].astype(BF16)
            w3b[...] = w3_ref[0].astype(BF16)

        x = x_ref[...].astype(BF16)
        a = _dot(x, w1b[...])
        b = _dot(x, w3b[...])
        o_ref[...] = (a * jax.nn.sigmoid(a) * b).astype(o_ref.dtype)

    @pl.when(t >= na_ref[0])
    def _():
        o_ref[...] = jnp.zeros_like(o_ref)


def _ffn_a(tile_expert, n_active, xs, w1, w3):
    n_rows, d = xs.shape
    de = w1.shape[2]
    ck = _tile(de, 512)
    n_tiles = n_rows // MOE_TILE

    def row_idx(c, t, te, na):
        return (jnp.minimum(t, na[0] - 1), 0)

    def w_idx(c, t, te, na):
        return (te[jnp.minimum(t, na[0] - 1)], 0, c)

    wspec = pl.BlockSpec((1, d, ck), w_idx)
    return pl.pallas_call(
        _ffn_a_kernel,
        grid_spec=pltpu.PrefetchScalarGridSpec(
            num_scalar_prefetch=2,
            grid=(de // ck, n_tiles),
            in_specs=[pl.BlockSpec((MOE_TILE, d), row_idx), wspec, wspec],
            out_specs=pl.BlockSpec((MOE_TILE, ck), lambda c, t, te, na: (t, c)),
            scratch_shapes=[pltpu.VMEM((d, ck), BF16), pltpu.VMEM((d, ck), BF16)]),
        out_shape=jax.ShapeDtypeStruct((n_rows, de), BF16),
        compiler_params=_cparams("arbitrary", "arbitrary"),
        name="ffn_a",
    )(tile_expert, n_active, xs, w1, w3)


def _ffn_b_kernel(te_ref, na_ref, h_ref, w2_ref, o_ref, w2b):
    t = pl.program_id(0)

    @pl.when(t < na_ref[0])
    def _():
        changed = jnp.logical_or(t == 0, te_ref[t] != te_ref[jnp.maximum(t - 1, 0)])

        @pl.when(changed)
        def _():
            w2b[...] = w2_ref[0].astype(BF16)

        o_ref[...] = _dot(h_ref[...], w2b[...])

    @pl.when(t >= na_ref[0])
    def _():
        o_ref[...] = jnp.zeros_like(o_ref)


def _ffn_b(tile_expert, n_active, hid, w2):
    n_rows, de = hid.shape
    d = w2.shape[2]
    n_tiles = n_rows // MOE_TILE

    def row_idx(t, te, na):
        return (jnp.minimum(t, na[0] - 1), 0)

    def w_idx(t, te, na):
        return (te[jnp.minimum(t, na[0] - 1)], 0, 0)

    return pl.pallas_call(
        _ffn_b_kernel,
        grid_spec=pltpu.PrefetchScalarGridSpec(
            num_scalar_prefetch=2,
            grid=(n_tiles,),
            in_specs=[pl.BlockSpec((MOE_TILE, de), row_idx),
                      pl.BlockSpec((1, de, d), w_idx)],
            out_specs=pl.BlockSpec((MOE_TILE, d), lambda t, te, na: (t, 0)),
            scratch_shapes=[pltpu.VMEM((de, d), BF16)]),
        out_shape=jax.ShapeDtypeStruct((n_rows, d), F32),
        compiler_params=_cparams("arbitrary"),
        name="ffn_b",
    )(tile_expert, n_active, hid, w2)


def _combine_kernel(p1_ref, p2_ref, y_ref, x1_ref, wf_ref, g2_ref, lng_ref, lnb_ref, o_ref, ybuf, sem, *, tm):
    i = pl.program_id(0)
    n = pl.num_programs(0)

    def issue(tile, slot):
        def body(r, carry):
            t = tile * tm + r
            pltpu.make_async_copy(y_ref.at[pl.ds(p1_ref[t], 1)], ybuf.at[slot, 0, pl.ds(r, 1)],
                                  sem.at[slot]).start()
            pltpu.make_async_copy(y_ref.at[pl.ds(p2_ref[t], 1)], ybuf.at[slot, 1, pl.ds(r, 1)],
                                  sem.at[slot]).start()
            return carry

        lax.fori_loop(0, tm, body, 0, unroll=8)

    @pl.when(i == 0)
    def _():
        issue(0, 0)

    @pl.when(i + 1 < n)
    def _():
        issue(i + 1, (i + 1) % 2)

    slot = i % 2
    for k in range(2):
        pltpu.make_async_copy(y_ref.at[pl.ds(0, tm)], ybuf.at[slot, k], sem.at[slot]).wait()

    wf = wf_ref[...]
    w1 = wf[:, 0:1]
    w2 = wf[:, 1:2]
    ffn = w1 * ybuf[slot, 0] + w2 * ybuf[slot, 1]
    o_ref[...] = _ln(DN_ALPHA * x1_ref[...] + g2_ref[...] * ffn) * lng_ref[...] + lnb_ref[...]


def _combine(pos1, pos2, y, x1, wf, gate2, ln_g, ln_b):
    s, d = x1.shape
    tm = _tile(s, 256, SUBLANES)
    row = pl.BlockSpec((1, d), lambda i, p1, p2: (0, 0))
    kern = functools.partial(_combine_kernel, tm=tm)
    return pl.pallas_call(
        kern,
        grid_spec=pltpu.PrefetchScalarGridSpec(
            num_scalar_prefetch=2,
            grid=(s // tm,),
            in_specs=[pl.BlockSpec(memory_space=pl.ANY),
                      pl.BlockSpec((tm, d), lambda i, p1, p2: (i, 0)),
                      pl.BlockSpec((tm, SUBLANES), lambda i, p1, p2: (i, 0)),
                      row, row, row],
            out_specs=pl.BlockSpec((tm, d), lambda i, p1, p2: (i, 0)),
            scratch_shapes=[pltpu.VMEM((2, 2, tm, d), F32), pltpu.SemaphoreType.DMA((2,))]),
        out_shape=jax.ShapeDtypeStruct((s, d), F32),
        compiler_params=_cparams("arbitrary"),
        name="combine",
    )(pos1, pos2, y, x1, wf, gate2, ln_g, ln_b)


def _layer(x2, c, w_ada, b_ada, w_in, b_if, conv_w, conv_b, head_norm_g, w_branch_sb, w_branch_ml, w_out,
           ln1_g, ln1_b, w_rg, b_rg, w_re, b_re, w1, w3, w2, ln2_g, ln2_b):
    s, d = x2.shape
    sbw = SB_HEADS * SB_HEAD_DIM
    mlw = ML_HEADS * ML_V_DIM
    qkw2 = 2 * ML_HEADS * ML_QK_DIM
    n_exp = N_GROUPS * EXPERTS_PER_GROUP

    mod = _mod(c.reshape(d, 1), w_ada, b_ada.reshape(1, -1))
    shift1, scale1, gate1, shift2, scale2, gate2 = [mod[:, i * d:(i + 1) * d] for i in range(N_MOD)]

    o_qk = 3 * sbw
    o_vo = o_qk + qkw2
    o_if = o_vo + 2 * mlw
    o_gates = o_if + 2 * ML_HEADS
    w_in_t = w_in.T

    u1, if_col, if_row = _ln1(x2, scale1, shift1, w_in_t[o_if:o_gates])
    qkv = _matmul_wt(u1, w_in_t, 0, 3 * sbw, BF16, "inproj_sb")
    qk_pre = _matmul_wt(u1, w_in_t, o_qk, qkw2, F32, "inproj_qk")
    vo = _matmul_wt(u1, w_in_t, o_vo, 2 * mlw, BF16, "inproj_vo")
    gates = _matmul_wt(u1, w_in_t, o_gates, 2 * d, BF16, "inproj_gates")

    y_sb = _attention(qkv)
    h_ml = _mlstm(qk_pre, vo, if_col, if_row, conv_w, conv_b, b_if, head_norm_g)
    merged = _merge(y_sb, h_ml, w_branch_sb.astype(BF16), w_branch_ml.astype(BF16), gates)
    mix = _matmul(merged, w_out.astype(BF16), F32, "outproj")

    w_router = jnp.zeros((d, LANES), F32).at[:, :N_GROUPS].set(w_rg).at[:, N_GROUPS:N_GROUPS + n_exp].set(w_re)
    b_router = jnp.zeros((1, LANES), F32).at[0, :N_GROUPS].set(b_rg).at[0, N_GROUPS:N_GROUPS + n_exp].set(b_re)
    x1, u2p, logits = _mid(mix, x2, gate1, ln1_g.reshape(1, d), ln1_b.reshape(1, d), scale2, shift2,
                           w_router, b_router)

    pos, wf, cnt = _route(logits)
    pos1, pos2 = pos[:, 0], pos[:, 1]
    counts = cnt[0, N_GROUPS:N_GROUPS + n_exp].astype(I32)
    tiles_per_expert = (counts + MOE_TILE - 1) // MOE_TILE
    tile_end = jnp.cumsum(tiles_per_expert)
    n_tiles = -(-(s * TOP_K_IN_GROUP) // MOE_TILE) + n_exp
    tile_expert = jnp.minimum(
        jnp.sum((tile_end[None, :] <= jnp.arange(n_tiles, dtype=I32)[:, None]).astype(I32), axis=1), n_exp - 1)
    n_active = tile_end[-1:]
    tail_tiles = n_active + jnp.arange(n_tiles - s * TOP_K_IN_GROUP // MOE_TILE, dtype=I32)
    zero_start = jnp.concatenate([jnp.where(counts > 0, (tile_end - 1) * MOE_TILE, -1),
                                  jnp.where(tail_tiles < n_tiles, tail_tiles * MOE_TILE, -1)])

    xs = _dispatch(pos1, pos2, zero_start, u2p, n_tiles * MOE_TILE)
    hid = _ffn_a(tile_expert, n_active, xs, w1, w3)
    y = _ffn_b(tile_expert, n_active, hid, w2)
    return _combine(pos1, pos2, y, x1, wf, gate2, ln2_g.reshape(1, d), ln2_b.reshape(1, d))


def kernel(x, c, w_ada, b_ada, w_in, b_if, conv_w, conv_b, head_norm_g, w_branch_sb, w_branch_ml, w_out,
           ln1_g, ln1_b, w_rg, b_rg, w_re, b_re, w1, w3, w2, ln2_g, ln2_b):
    bsz, seq, d = x.shape
    assert bsz == 1 and w_ada.shape[0] == DEPTH
    x2 = x.reshape(seq, d)
    for l in range(DEPTH):
        x2 = _layer(x2, c, w_ada[l], b_ada[l], w_in[l], b_if[l], conv_w[l], conv_b[l], head_norm_g[l],
                    w_branch_sb[l], w_branch_ml[l], w_out[l], ln1_g[l], ln1_b[l], w_rg[l], b_rg[l],
                    w_re[l], b_re[l], w1[l], w3[l], w2[l], ln2_g[l], ln2_b[l])
    return x2.reshape(bsz, seq, d)
```

```python
import functools

import jax
import jax.numpy as jnp
from jax import lax
from jax.experimental import pallas as pl
from jax.experimental.pallas import tpu as pltpu

F32 = jnp.float32
BF16 = jnp.bfloat16
I32 = jnp.int32

DEPTH = 1
SB_HEADS = 16
SB_HEAD_DIM = 128
ML_HEADS = 8
ML_V_DIM = 256
ML_QK_DIM = 128
CONV_WIDTH = 4
N_GROUPS = 4
EXPERTS_PER_GROUP = 8
TOP_K_IN_GROUP = 2
N_MOD = 6
DN_ALPHA = (2.0 * DEPTH) ** 0.25
LN_EPS = 1e-5

LANES = 128
SUBLANES = 8
VMEM_LIMIT_BYTES = 60 * 1024 * 1024

EXP_UNDERFLOW = -104.0
OUT_OF_RANGE = -1e30

MOE_TILE = 256
ATTN_HEADS_PER_STEP = 2
ATTN_WALK_UNROLL = 2


def _tile(n, pref, align=LANES):
    if n <= pref:
        return n
    t = (pref // align) * align
    while n % t:
        t -= align
    return t


def _cparams(*sem):
    return pltpu.CompilerParams(dimension_semantics=tuple(sem), vmem_limit_bytes=VMEM_LIMIT_BYTES)


def _ln(x):
    mu = jnp.mean(x, axis=-1, keepdims=True)
    xc = x - mu
    var = jnp.mean(xc * xc, axis=-1, keepdims=True)
    return xc * lax.rsqrt(var + LN_EPS)


def _split2(x):
    hi = x.astype(BF16)
    lo = (x - hi.astype(F32)).astype(BF16)
    return hi, lo


def _split3(x):
    hi = x.astype(BF16)
    r = x - hi.astype(F32)
    mid = r.astype(BF16)
    lo = (r - mid.astype(F32)).astype(BF16)
    return hi, mid, lo


def _dot(a, b):
    return jnp.dot(a, b, preferred_element_type=F32)


def _dot_nt(a, b):
    return lax.dot_general(a, b, (((1,), (1,)), ((), ())), preferred_element_type=F32)


def _dot_tn(a, b):
    return lax.dot_general(a, b, (((0,), (0,)), ((), ())), preferred_element_type=F32)


def _softplus(z):
    return jnp.maximum(z, 0.0) + jnp.log(1.0 + jnp.exp(-jnp.abs(z)))


def _mod_kernel(c_ref, w_ref, b_ref, o_ref):
    @pl.when(pl.program_id(1) == 0)
    def _():
        o_ref[...] = b_ref[...]

    c = c_ref[...]
    s = c * jax.nn.sigmoid(c)
    o_ref[...] += jnp.sum(w_ref[...] * s, axis=0, keepdims=True)


def _mod(c_col, w_ada, b_ada):
    d, n = w_ada.shape
    tk = _tile(d, 512, SUBLANES)
    tn = _tile(n, 2048)
    return pl.pallas_call(
        _mod_kernel,
        grid=(n // tn, d // tk),
        in_specs=[pl.BlockSpec((tk, 1), lambda j, k: (k, 0)),
                  pl.BlockSpec((tk, tn), lambda j, k: (k, j)),
                  pl.BlockSpec((1, tn), lambda j, k: (0, j))],
        out_specs=pl.BlockSpec((1, tn), lambda j, k: (0, j)),
        out_shape=jax.ShapeDtypeStruct((1, n), F32),
        compiler_params=_cparams("parallel", "arbitrary"),
        name="mod",
    )(c_col, w_ada, b_ada)


def _ln1_kernel(x_ref, sc_ref, sh_ref, wh_ref, wl_ref, wth_ref, wtl_ref, u_ref, if_ref, ift_ref):
    u = _ln(x_ref[...]) * (1.0 + sc_ref[...]) + sh_ref[...]
    u_ref[...] = u.astype(BF16)
    uh, ul = _split2(u)
    if_ref[...] = _dot(uh, wh_ref[...]) + _dot(ul, wh_ref[...]) + _dot(uh, wl_ref[...])
    ift_ref[...] = (_dot_nt(wth_ref[...], uh) + _dot_nt(wth_ref[...], ul) + _dot_nt(wtl_ref[...], uh))


def _ln1(x2, scale, shift, w_if_t):
    s, d = x2.shape
    tm = _tile(s, 256, SUBLANES)
    wpad_t = jnp.zeros((LANES, d), F32).at[:w_if_t.shape[0], :].set(w_if_t)
    wth, wtl = _split2(wpad_t)
    wh, wl = wth.T, wtl.T
    row = pl.BlockSpec((1, d), lambda i: (0, 0))
    wspec = pl.BlockSpec((d, LANES), lambda i: (0, 0))
    wtspec = pl.BlockSpec((LANES, d), lambda i: (0, 0))
    return pl.pallas_call(
        _ln1_kernel,
        grid=(s // tm,),
        in_specs=[pl.BlockSpec((tm, d), lambda i: (i, 0)), row, row, wspec, wspec, wtspec, wtspec],
        out_specs=[pl.BlockSpec((tm, d), lambda i: (i, 0)),
                   pl.BlockSpec((tm, LANES), lambda i: (i, 0)),
                   pl.BlockSpec((LANES, tm), lambda i: (0, i))],
        out_shape=[jax.ShapeDtypeStruct((s, d), BF16),
                   jax.ShapeDtypeStruct((s, LANES), F32),
                   jax.ShapeDtypeStruct((LANES, s), F32)],
        compiler_params=_cparams("parallel"),
        name="ln1",
    )(x2, scale, shift, wh, wl, wth, wtl)


def _mm_kernel(a_ref, b_ref, o_ref):
    o_ref[...] = _dot(a_ref[...], b_ref[...]).astype(o_ref.dtype)


def _matmul(a, b, out_dtype, name, col0=0, ncols=None):
    m, k = a.shape
    n = b.shape[1] if ncols is None else ncols
    tm = _tile(m, 1024, SUBLANES)
    tn = _tile(n, 1024)
    j0 = col0 // tn
    assert j0 * tn == col0
    return pl.pallas_call(
        _mm_kernel,
        grid=(m // tm, n // tn),
        in_specs=[pl.BlockSpec((tm, k), lambda i, j: (i, 0)),
                  pl.BlockSpec((k, tn), lambda i, j: (0, j0 + j))],
        out_specs=pl.BlockSpec((tm, tn), lambda i, j: (i, j)),
        out_shape=jax.ShapeDtypeStruct((m, n), out_dtype),
        compiler_params=_cparams("parallel", "arbitrary"),
        name=name,
    )(a, b)


def _mm_wt_kernel(a_ref, w_ref, o_ref, wb):
    @pl.when(pl.program_id(1) == 0)
    def _():
        wb[...] = w_ref[...].astype(BF16).T

    o_ref[...] = _dot(a_ref[...], wb[...]).astype(o_ref.dtype)


def _matmul_wt(a, wt, row0, nrows, out_dtype, name):
    m, k = a.shape
    tm = _tile(m, 1024, SUBLANES)
    tn = _tile(nrows, 512)
    assert row0 % SUBLANES == 0
    return pl.pallas_call(
        _mm_wt_kernel,
        grid=(nrows // tn, m // tm),
        in_specs=[pl.BlockSpec((tm, k), lambda j, i: (i, 0)),
                  pl.BlockSpec((pl.Element(tn), pl.Element(k)), lambda j, i: (pl.multiple_of(row0 + j * tn, SUBLANES), 0))],
        out_specs=pl.BlockSpec((tm, tn), lambda j, i: (i, j)),
        out_shape=jax.ShapeDtypeStruct((m, nrows), out_dtype),
        scratch_shapes=[pltpu.VMEM((k, tn), BF16)],
        compiler_params=_cparams("arbitrary", "arbitrary"),
        name=name,
    )(a, wt)


def _attn_kernel(q_ref, k_ref, v_ref, o_ref, *, tq, scale):
    qb = pl.program_id(1)
    dh = SB_HEAD_DIM
    nh = q_ref.shape[1] // dh
    row = lax.broadcasted_iota(I32, (tq, tq), 0)
    col = lax.broadcasted_iota(I32, (tq, tq), 1)
    strict = col < row
    later = (row > col).astype(BF16)
    qs = [q_ref[:, i * dh:(i + 1) * dh] for i in range(nh)]

    def tile(i, kb, c, diagonal):
        start = pl.multiple_of(kb * tq, tq)
        k = k_ref[pl.ds(start, tq), i * dh:(i + 1) * dh]
        v = v_ref[pl.ds(start, tq), i * dh:(i + 1) * dh]
        z = _dot_nt(qs[i], k) * scale
        log_keep = -_softplus(z)
        if diagonal:
            log_keep = jnp.where(strict, log_keep, 0.0)
        hi, lo = _split2(log_keep)
        between = _dot(hi, later) + _dot(lo, later) + c
        w = jnp.exp(z + log_keep + between)
        if diagonal:
            w = jnp.where(strict, w, 0.0)
        pv = _dot(w.astype(BF16), v)
        return pv, c + jnp.sum(log_keep, axis=1, keepdims=True)

    zero_c = jnp.zeros((tq, 1), F32)
    first = [tile(i, qb, zero_c, True) for i in range(nh)]
    accs = tuple(p for p, _ in first)
    cs = tuple(c for _, c in first)

    def cmax_of(cs):
        m = jnp.max(cs[0])
        for c in cs[1:]:
            m = jnp.maximum(m, jnp.max(c))
        return m

    def cond(st):
        j, _, _, cmax = st
        return jnp.logical_and(j <= qb, cmax > EXP_UNDERFLOW)

    def body(st):
        j, accs, cs, _ = st
        accs, cs = list(accs), list(cs)
        for u in range(ATTN_WALK_UNROLL):
            kb = qb - j - u
            in_range = kb >= 0
            kb = jnp.maximum(kb, 0)
            for i in range(nh):
                c_in = cs[i] if u == 0 else jnp.where(in_range, cs[i], OUT_OF_RANGE)
                pv, c_out = tile(i, kb, c_in, False)
                accs[i] = accs[i] + pv
                cs[i] = c_out
        return j + ATTN_WALK_UNROLL, tuple(accs), tuple(cs), cmax_of(cs)

    _, accs, _, _ = lax.while_loop(cond, body, (jnp.int32(1), accs, cs, cmax_of(cs)))
    for i in range(nh):
        o_ref[:, i * dh:(i + 1) * dh] = accs[i].astype(o_ref.dtype)


def _attention(qkv):
    s = qkv.shape[0]
    tq = _tile(s, 256, SUBLANES)
    dh = SB_HEAD_DIM
    hps = ATTN_HEADS_PER_STEP
    groups = SB_HEADS // hps
    w = hps * dh
    kern = functools.partial(_attn_kernel, tq=tq, scale=dh ** -0.5)
    return pl.pallas_call(
        kern,
        grid=(groups, s // tq),
        in_specs=[pl.BlockSpec((tq, w), lambda g, i: (i, g)),
                  pl.BlockSpec((s, w), lambda g, i: (0, groups + g)),
                  pl.BlockSpec((s, w), lambda g, i: (0, 2 * groups + g))],
        out_specs=pl.BlockSpec((tq, w), lambda g, i: (i, g)),
        out_shape=jax.ShapeDtypeStruct((s, SB_HEADS * dh), BF16),
        compiler_params=_cparams("parallel", "arbitrary"),
        name="attn",
    )(qkv, qkv, qkv)


def _mlstm_kernel(qk_ref, v_ref, o_ref, if_ref, ift_ref, cw_ref, cb_ref, brow_ref, bcol_ref, hg_ref,
                  out_ref, xbuf, ct_ref, nt_ref, m_ref, *, tr):
    heads, dk, dv = ML_HEADS, ML_QK_DIM, ML_V_DIM
    qkw = heads * dk
    halo = SUBLANES

    @pl.when(pl.program_id(0) == 0)
    def _():
        xbuf[0:halo, :] = jnp.zeros((halo, 2 * qkw), F32)
        ct_ref[...] = jnp.zeros_like(ct_ref)
        nt_ref[...] = jnp.zeros_like(nt_ref)
        m_ref[...] = jnp.zeros_like(m_ref)

    xbuf[halo:halo + tr, :] = qk_ref[...]
    y = cb_ref[...]
    for i in range(CONV_WIDTH):
        y = y + cw_ref[i:i + 1, :] * xbuf[pl.ds(halo - (CONV_WIDTH - 1) + i, tr), :]
    xbuf[0:halo, :] = xbuf[tr:tr + halo, :]
    qk = y * jax.nn.sigmoid(y)

    g_col = if_ref[...] + brow_ref[...]
    g_row = ift_ref[...] + bcol_ref[...]
    lf_col = -_softplus(-g_col)
    lf_row = -_softplus(-g_row)
    r = lax.broadcasted_iota(I32, (tr, tr), 0)
    s = lax.broadcasted_iota(I32, (tr, tr), 1)
    causal = s <= r
    incl_col = causal.astype(BF16)
    incl_row = (r <= s).astype(BF16)
    c1, c2, c3 = _split3(lf_col)
    b_col_all = _dot(incl_col, c1) + _dot(incl_col, c2) + _dot(incl_col, c3)
    r1, r2, r3 = _split3(lf_row)
    b_row_all = _dot(r1, incl_row) + _dot(r2, incl_row) + _dot(r3, incl_row)

    for h in range(heads):
        q = qk[:, h * dk:(h + 1) * dk].astype(BF16)
        k = (qk[:, qkw + h * dk:qkw + (h + 1) * dk] * (dk ** -0.5)).astype(BF16)
        v = v_ref[:, h * dv:(h + 1) * dv]
        ig_col = g_col[:, h:h + 1]
        b_col = b_col_all[:, heads + h:heads + h + 1]
        ig_row = g_row[h:h + 1, :]
        b_row = b_row_all[heads + h:heads + h + 1, :]
        g_tot = b_row[:, tr - 1:tr]
        m_prev = m_ref[h][:, 0:1]

        dmat = jnp.where(causal, b_col - b_row + ig_row, -jnp.inf)
        m_inter = b_col + m_prev
        m_row = jnp.maximum(m_inter, jnp.max(dmat, axis=1, keepdims=True))
        p = jnp.exp(dmat - m_row) * _dot_nt(q, k)
        inter_scale = jnp.exp(m_inter - m_row)
        inter_num = _dot(q, ct_ref[h].astype(BF16))
        inter_den = _dot(q, nt_ref[h].astype(BF16))[:, 0:1]
        num = _dot(p.astype(BF16), v) + inter_scale * inter_num
        den = jnp.sum(p, axis=1, keepdims=True) + inter_scale * inter_den
        hh = num / jnp.maximum(jnp.abs(den), jnp.exp(-m_row))

        a_row = g_tot - b_row + ig_row
        a_col = g_tot - b_col + ig_col
        m_new = jnp.maximum(g_tot + m_prev, jnp.max(a_row, axis=1, keepdims=True))
        decay = jnp.exp(g_tot + m_prev - m_new)
        w_col = jnp.exp(a_col - m_new)
        ct_ref[h] = decay * ct_ref[h] + _dot_tn(k, (w_col * v.astype(F32)).astype(BF16))
        nt_ref[h] = decay * nt_ref[h] + _dot_tn(k, jnp.broadcast_to(w_col, (tr, LANES)).astype(BF16))
        m_ref[h] = jnp.broadcast_to(m_new, (1, LANES))

        hn = _ln(hh) * hg_ref[:, h * dv:(h + 1) * dv]
        gate = jax.nn.sigmoid(o_ref[:, h * dv:(h + 1) * dv].astype(F32))
        out_ref[:, h * dv:(h + 1) * dv] = (gate * hn).astype(out_ref.dtype)


def _mlstm(qk_pre, vo, if_col, if_row, conv_w, conv_b, b_if, head_norm_g):
    s = qk_pre.shape[0]
    tr = _tile(s, 256, SUBLANES)
    heads, dk, dv = ML_HEADS, ML_QK_DIM, ML_V_DIM
    mlw = heads * dv
    qkw2 = 2 * heads * dk
    brow = jnp.zeros((1, LANES), F32).at[0, :2 * heads].set(b_if)
    full = lambda shape: pl.BlockSpec(shape, lambda i: (0,) * len(shape))
    kern = functools.partial(_mlstm_kernel, tr=tr)
    return pl.pallas_call(
        kern,
        grid=(s // tr,),
        in_specs=[pl.BlockSpec((tr, qkw2), lambda i: (i, 0)),
                  pl.BlockSpec((tr, mlw), lambda i: (i, 0)),
                  pl.BlockSpec((tr, mlw), lambda i: (i, 1)),
                  pl.BlockSpec((tr, LANES), lambda i: (i, 0)),
                  pl.BlockSpec((LANES, tr), lambda i: (0, i)),
                  full((CONV_WIDTH, qkw2)), full((1, qkw2)), full((1, LANES)), full((LANES, 1)),
                  full((1, mlw))],
        out_specs=pl.BlockSpec((tr, mlw), lambda i: (i, 0)),
        out_shape=jax.ShapeDtypeStruct((s, mlw), BF16),
        scratch_shapes=[pltpu.VMEM((tr + 2 * SUBLANES, qkw2), F32),
                        pltpu.VMEM((heads, dk, dv), F32),
                        pltpu.VMEM((heads, dk, LANES), F32),
                        pltpu.VMEM((heads, 1, LANES), F32)],
        compiler_params=_cparams("arbitrary"),
        name="mlstm",
    )(qk_pre, vo, vo, if_col, if_row, conv_w, conv_b.reshape(1, -1), brow, brow.reshape(LANES, 1),
      head_norm_g.reshape(1, -1))


def _merge_kernel(y_ref, h_ref, wsb_ref, wml_ref, gsb_ref, gml_ref, o_ref):
    a = _dot(y_ref[...], wsb_ref[...])
    b = _dot(h_ref[...], wml_ref[...])
    o = jax.nn.sigmoid(gsb_ref[...].astype(F32)) * a + jax.nn.sigmoid(gml_ref[...].astype(F32)) * b
    o_ref[...] = o.astype(o_ref.dtype)


def _merge(y_sb, h_ml, w_sb, w_ml, gates):
    s = y_sb.shape[0]
    d = w_sb.shape[1]
    tm = _tile(s, 512, SUBLANES)
    tn = _tile(d, 1024)
    nj = d // tn
    return pl.pallas_call(
        _merge_kernel,
        grid=(s // tm, nj),
        in_specs=[pl.BlockSpec((tm, y_sb.shape[1]), lambda i, j: (i, 0)),
                  pl.BlockSpec((tm, h_ml.shape[1]), lambda i, j: (i, 0)),
                  pl.BlockSpec((w_sb.shape[0], tn), lambda i, j: (0, j)),
                  pl.BlockSpec((w_ml.shape[0], tn), lambda i, j: (0, j)),
                  pl.BlockSpec((tm, tn), lambda i, j: (i, j)),
                  pl.BlockSpec((tm, tn), lambda i, j: (i, nj + j))],
        out_specs=pl.BlockSpec((tm, tn), lambda i, j: (i, j)),
        out_shape=jax.ShapeDtypeStruct((s, d), BF16),
        compiler_params=_cparams("parallel", "arbitrary"),
        name="merge",
    )(y_sb, h_ml, w_sb, w_ml, gates, gates)


def _mid_kernel(mix_ref, x_ref, g1_ref, lng_ref, lnb_ref, sc2_ref, sh2_ref, wrh_ref, wrl_ref, br_ref,
                x1_ref, u2_ref, lg_ref):
    x1 = _ln(DN_ALPHA * x_ref[...] + g1_ref[...] * mix_ref[...]) * lng_ref[...] + lnb_ref[...]
    x1_ref[...] = x1
    u2 = _ln(x1) * (1.0 + sc2_ref[...]) + sh2_ref[...]
    u2_ref[...] = u2.astype(BF16)
    uh, ul = _split2(u2)
    lg_ref[...] = (_dot(uh, wrh_ref[...]) + _dot(ul, wrh_ref[...]) + _dot(uh, wrl_ref[...]) + br_ref[...])


def _mid(mix, x2, gate1, ln_g, ln_b, scale2, shift2, w_router, b_router):
    s, d = x2.shape
    tm = _tile(s, 256, SUBLANES)
    wrh, wrl = _split2(w_router)
    row = pl.BlockSpec((1, d), lambda i: (0, 0))
    wr = pl.BlockSpec((d, LANES), lambda i: (0, 0))
    tile = pl.BlockSpec((tm, d), lambda i: (i, 0))
    return pl.pallas_call(
        _mid_kernel,
        grid=(s // tm,),
        in_specs=[tile, tile, row, row, row, row, row, wr, wr, pl.BlockSpec((1, LANES), lambda i: (0, 0))],
        out_specs=[tile, tile, pl.BlockSpec((tm, LANES), lambda i: (i, 0))],
        out_shape=[jax.ShapeDtypeStruct((s, d), F32),
                   jax.ShapeDtypeStruct((s, d), BF16),
                   jax.ShapeDtypeStruct((s, LANES), F32)],
        compiler_params=_cparams("parallel"),
        name="mid",
    )(mix, x2, gate1, ln_g, ln_b, scale2, shift2, wrh, wrl, b_router)


def _route_kernel(lg_ref, pos_ref, wf_ref, cnt_ref, carry_ref, pstart_ref, *, tm):
    phase = pl.program_id(0)
    i = pl.program_id(1)
    ng, epg = N_GROUPS, EXPERTS_PER_GROUP

    @pl.when(jnp.logical_and(phase == 0, i == 0))
    def _():
        carry_ref[...] = jnp.zeros_like(carry_ref)

    @pl.when(jnp.logical_and(phase == 1, i == 0))
    def _():
        counts = carry_ref[...]
        cnt_ref[...] = counts
        n_tiles = jnp.ceil(counts * (1.0 / MOE_TILE))
        a = lax.broadcasted_iota(I32, (LANES, LANES), 0)
        b = lax.broadcasted_iota(I32, (LANES, LANES), 1)
        tiles_before = _dot(jnp.broadcast_to(n_tiles, (SUBLANES, LANES)).astype(BF16), (a < b).astype(BF16))
        pstart_ref[...] = tiles_before[0:1, :] * float(MOE_TILE)
        carry_ref[...] = jnp.zeros_like(carry_ref)

    lg = lg_ref[...]
    lane = lax.broadcasted_iota(I32, lg.shape, 1)
    lane_f = lane.astype(F32)
    neg = -jnp.inf
    big = float(LANES)

    def first_argmax(vals, vmax):
        return jnp.min(jnp.where(vals == vmax, lane_f, big), axis=1, keepdims=True)

    gmask = lane < ng
    gl = jnp.where(gmask, lg, neg)
    gmax = jnp.max(gl, axis=1, keepdims=True)
    gidx = first_argmax(gl, gmax).astype(I32)
    lo = ng + gidx * epg
    el = jnp.where(jnp.logical_and(lane >= lo, lane < lo + epg), lg, neg)
    t1 = jnp.max(el, axis=1, keepdims=True)
    i1 = first_argmax(el, t1).astype(I32)
    el2 = jnp.where(lane == i1, neg, el)
    t2 = jnp.max(el2, axis=1, keepdims=True)
    i2 = first_argmax(el2, t2).astype(I32)
    hit1 = lane == i1
    hit2 = lane == i2
    onehot = jnp.logical_or(hit1, hit2)

    @pl.when(phase == 1)
    def _():
        gsum = jnp.sum(jnp.where(gmask, jnp.exp(lg - gmax), 0.0), axis=1, keepdims=True)
        g_w = 1.0 / gsum
        ex = jnp.exp(t2 - t1)
        w1 = g_w / (1.0 + ex)
        w2 = g_w * ex / (1.0 + ex)
        r = lax.broadcasted_iota(I32, (tm, tm), 0)
        c = lax.broadcasted_iota(I32, (tm, tm), 1)
        row_of = (_dot((c < r).astype(BF16), onehot.astype(BF16)) + carry_ref[...] + pstart_ref[...])
        pos1 = jnp.sum(jnp.where(hit1, row_of, 0.0), axis=1, keepdims=True).astype(I32)
        pos2 = jnp.sum(jnp.where(hit2, row_of, 0.0), axis=1, keepdims=True).astype(I32)
        pos_ref[...] = jnp.where(lane == 0, pos1, pos2)[:, :SUBLANES]
        wf_ref[...] = jnp.where(lane == 0, w1, w2)[:, :SUBLANES]

    carry_ref[...] += jnp.sum(onehot.astype(F32), axis=0, keepdims=True)


def _route(logits):
    s = logits.shape[0]
    tm = _tile(s, 512, SUBLANES)
    kern = functools.partial(_route_kernel, tm=tm)
    return pl.pallas_call(
        kern,
        grid=(2, s // tm),
        in_specs=[pl.BlockSpec((tm, LANES), lambda p, i: (i, 0))],
        out_specs=[pl.BlockSpec((tm, SUBLANES), lambda p, i: (i * p, 0)),
                   pl.BlockSpec((tm, SUBLANES), lambda p, i: (i * p, 0)),
                   pl.BlockSpec((1, LANES), lambda p, i: (0, 0))],
        out_shape=[jax.ShapeDtypeStruct((s, SUBLANES), I32),
                   jax.ShapeDtypeStruct((s, SUBLANES), F32),
                   jax.ShapeDtypeStruct((1, LANES), F32)],
        scratch_shapes=[pltpu.VMEM((1, LANES), F32), pltpu.VMEM((1, LANES), F32)],
        compiler_params=_cparams("arbitrary", "arbitrary"),
        name="route",
    )(logits)


def _dispatch_kernel(p1_ref, p2_ref, zs_ref, x_ref, dst_ref, stage, zbuf, sem, zsem, *, tm, n_zero):
    i = pl.program_id(0)
    n = pl.num_programs(0)
    slot = i % 2

    def zero_copy(e):
        start = pl.multiple_of(zs_ref[e], MOE_TILE)
        return pltpu.make_async_copy(zbuf, dst_ref.at[pl.ds(start, MOE_TILE)], zsem)

    def drain(s):
        for _ in range(TOP_K_IN_GROUP):
            pltpu.make_async_copy(stage.at[s], dst_ref.at[pl.ds(0, tm)], sem.at[s]).wait()

    @pl.when(i == 0)
    def _():
        zbuf[...] = jnp.zeros_like(zbuf)
        for e in range(n_zero):
            pl.when(zs_ref[e] >= 0)(lambda e=e: zero_copy(e).start())
        for e in range(n_zero):
            pl.when(zs_ref[e] >= 0)(lambda e=e: zero_copy(e).wait())

    @pl.when(i >= 2)
    def _():
        drain(slot)

    stage[slot] = x_ref[...].astype(F32)

    def body(r, carry):
        t = i * tm + r
        src = stage.at[slot, pl.ds(r, 1)]
        pltpu.make_async_copy(src, dst_ref.at[pl.ds(p1_ref[t], 1)], sem.at[slot]).start()
        pltpu.make_async_copy(src, dst_ref.at[pl.ds(p2_ref[t], 1)], sem.at[slot]).start()
        return carry

    lax.fori_loop(0, tm, body, 0, unroll=8)

    @pl.when(i == n - 1)
    def _():
        drain(slot)

        @pl.when(n >= 2)
        def _():
            drain(1 - slot)


def _dispatch(pos1, pos2, zero_start, u2p, n_rows):
    s, w = u2p.shape
    tm = _tile(s, 256, SUBLANES)
    kern = functools.partial(_dispatch_kernel, tm=tm, n_zero=zero_start.shape[0])
    return pl.pallas_call(
        kern,
        grid_spec=pltpu.PrefetchScalarGridSpec(
            num_scalar_prefetch=3,
            grid=(s // tm,),
            in_specs=[pl.BlockSpec((tm, w), lambda i, p1, p2, zs: (i, 0))],
            out_specs=pl.BlockSpec(memory_space=pl.ANY),
            scratch_shapes=[pltpu.VMEM((2, tm, w), F32), pltpu.VMEM((MOE_TILE, w), F32),
                            pltpu.SemaphoreType.DMA((2,)), pltpu.SemaphoreType.DMA]),
        out_shape=jax.ShapeDtypeStruct((n_rows, w), F32),
        compiler_params=_cparams("arbitrary"),
        name="dispatch",
    )(pos1, pos2, zero_start, u2p)


def _ffn_a_kernel(te_ref, na_ref, x_ref, w1_ref, w3_ref, o_ref, w1b, w3b):
    t = pl.program_id(1)

    @pl.when(t < na_ref[0])
    def _():
        changed = jnp.logical_or(t == 0, te_ref[t] != te_ref[jnp.maximum(t - 1, 0)])

        @pl.when(changed)
        def _():
            w1b[...] = w1_ref[0].astype(BF16)
            w3b[...] = w3_ref[0].astype(BF16)

        x = x_ref[...].astype(BF16)
        a = _dot(x, w1b[...])
        b = _dot(x, w3b[...])
        o_ref[...] = (a * jax.nn.sigmoid(a) * b).astype(o_ref.dtype)

    @pl.when(t >= na_ref[0])
    def _():
        o_ref[...] = jnp.zeros_like(o_ref)


def _ffn_a(tile_expert, n_active, xs, w1, w3):
    n_rows, d = xs.shape
    de = w1.shape[2]
    ck = _tile(de, 512)
    n_tiles = n_rows // MOE_TILE

    def row_idx(c, t, te, na):
        return (jnp.minimum(t, na[0] - 1), 0)

    def w_idx(c, t, te, na):
        return (te[jnp.minimum(t, na[0] - 1)], 0, c)

    wspec = pl.BlockSpec((1, d, ck), w_idx)
    return pl.pallas_call(
        _ffn_a_kernel,
        grid_spec=pltpu.PrefetchScalarGridSpec(
            num_scalar_prefetch=2,
            grid=(de // ck, n_tiles),
            in_specs=[pl.BlockSpec((MOE_TILE, d), row_idx), wspec, wspec],
            out_specs=pl.BlockSpec((MOE_TILE, ck), lambda c, t, te, na: (t, c)),
            scratch_shapes=[pltpu.VMEM((d, ck), BF16), pltpu.VMEM((d, ck), BF16)]),
        out_shape=jax.ShapeDtypeStruct((n_rows, de), BF16),
        compiler_params=_cparams("arbitrary", "arbitrary"),
        name="ffn_a",
    )(tile_expert, n_active, xs, w1, w3)


def _ffn_b_kernel(te_ref, na_ref, h_ref, w2_ref, o_ref, w2b):
    t = pl.program_id(0)

    @pl.when(t < na_ref[0])
    def _():
        changed = jnp.logical_or(t == 0, te_ref[t] != te_ref[jnp.maximum(t - 1, 0)])

        @pl.when(changed)
        def _():
            w2b[...] = w2_ref[0].astype(BF16)

        o_ref[...] = _dot(h_ref[...], w2b[...])

    @pl.when(t >= na_ref[0])
    def _():
        o_ref[...] = jnp.zeros_like(o_ref)


def _ffn_b(tile_expert, n_active, hid, w2):
    n_rows, de = hid.shape
    d = w2.shape[2]
    n_tiles = n_rows // MOE_TILE

    def row_idx(t, te, na):
        return (jnp.minimum(t, na[0] - 1), 0)

    def w_idx(t, te, na):
        return (te[jnp.minimum(t, na[0] - 1)], 0, 0)

    return pl.pallas_call(
        _ffn_b_kernel,
        grid_spec=pltpu.PrefetchScalarGridSpec(
            num_scalar_prefetch=2,
            grid=(n_tiles,),
            in_specs=[pl.BlockSpec((MOE_TILE, de), row_idx),
                      pl.BlockSpec((1, de, d), w_idx)],
            out_specs=pl.BlockSpec((MOE_TILE, d), lambda t, te, na: (t, 0)),
            scratch_shapes=[pltpu.VMEM((de, d), BF16)]),
        out_shape=jax.ShapeDtypeStruct((n_rows, d), F32),
        compiler_params=_cparams("arbitrary"),
        name="ffn_b",
    )(tile_expert, n_active, hid, w2)


def _combine_kernel(p1_ref, p2_ref, y_ref, x1_ref, wf_ref, g2_ref, lng_ref, lnb_ref, o_ref, ybuf, sem, *, tm):
    i = pl.program_id(0)
    n = pl.num_programs(0)

    def issue(tile, slot):
        def body(r, carry):
            t = tile * tm + r
            pltpu.make_async_copy(y_ref.at[pl.ds(p1_ref[t], 1)], ybuf.at[slot, 0, pl.ds(r, 1)],
                                  sem.at[slot]).start()
            pltpu.make_async_copy(y_ref.at[pl.ds(p2_ref[t], 1)], ybuf.at[slot, 1, pl.ds(r, 1)],
                                  sem.at[slot]).start()
            return carry

        lax.fori_loop(0, tm, body, 0, unroll=8)

    @pl.when(i == 0)
    def _():
        issue(0, 0)

    @pl.when(i + 1 < n)
    def _():
        issue(i + 1, (i + 1) % 2)

    slot = i % 2
    for k in range(2):
        pltpu.make_async_copy(y_ref.at[pl.ds(0, tm)], ybuf.at[slot, k], sem.at[slot]).wait()

    wf = wf_ref[...]
    w1 = wf[:, 0:1]
    w2 = wf[:, 1:2]
    ffn = w1 * ybuf[slot, 0] + w2 * ybuf[slot, 1]
    o_ref[...] = _ln(DN_ALPHA * x1_ref[...] + g2_ref[...] * ffn) * lng_ref[...] + lnb_ref[...]


def _combine(pos1, pos2, y, x1, wf, gate2, ln_g, ln_b):
    s, d = x1.shape
    tm = _tile(s, 256, SUBLANES)
    row = pl.BlockSpec((1, d), lambda i, p1, p2: (0, 0))
    kern = functools.partial(_combine_kernel, tm=tm)
    return pl.pallas_call(
        kern,
        grid_spec=pltpu.PrefetchScalarGridSpec(
            num_scalar_prefetch=2,
            grid=(s // tm,),
            in_specs=[pl.BlockSpec(memory_space=pl.ANY),
                      pl.BlockSpec((tm, d), lambda i, p1, p2: (i, 0)),
                      pl.BlockSpec((tm, SUBLANES), lambda i, p1, p2: (i, 0)),
                      row, row, row],
            out_specs=pl.BlockSpec((tm, d), lambda i, p1, p2: (i, 0)),
            scratch_shapes=[pltpu.VMEM((2, 2, tm, d), F32), pltpu.SemaphoreType.DMA((2,))]),
        out_shape=jax.ShapeDtypeStruct((s, d), F32),
        compiler_params=_cparams("arbitrary"),
        name="combine",
    )(pos1, pos2, y, x1, wf, gate2, ln_g, ln_b)


def _layer(x2, c, w_ada, b_ada, w_in, b_if, conv_w, conv_b, head_norm_g, w_branch_sb, w_branch_ml, w_out,
           ln1_g, ln1_b, w_rg, b_rg, w_re, b_re, w1, w3, w2, ln2_g, ln2_b):
    s, d = x2.shape
    sbw = SB_HEADS * SB_HEAD_DIM
    mlw = ML_HEADS * ML_V_DIM
    qkw2 = 2 * ML_HEADS * ML_QK_DIM
    n_exp = N_GROUPS * EXPERTS_PER_GROUP

    mod = _mod(c.reshape(d, 1), w_ada, b_ada.reshape(1, -1))
    shift1, scale1, gate1, shift2, scale2, gate2 = [mod[:, i * d:(i + 1) * d] for i in range(N_MOD)]

    o_qk = 3 * sbw
    o_vo = o_qk + qkw2
    o_if = o_vo + 2 * mlw
    o_gates = o_if + 2 * ML_HEADS
    w_in_t = w_in.T

    u1, if_col, if_row = _ln1(x2, scale1, shift1, w_in_t[o_if:o_gates])
    qkv = _matmul_wt(u1, w_in_t, 0, 3 * sbw, BF16, "inproj_sb")
    qk_pre = _matmul_wt(u1, w_in_t, o_qk, qkw2, F32, "inproj_qk")
    vo = _matmul_wt(u1, w_in_t, o_vo, 2 * mlw, BF16, "inproj_vo")
    gates = _matmul_wt(u1, w_in_t, o_gates, 2 * d, BF16, "inproj_gates")

    y_sb = _attention(qkv)
    h_ml = _mlstm(qk_pre, vo, if_col, if_row, conv_w, conv_b, b_if, head_norm_g)
    merged = _merge(y_sb, h_ml, w_branch_sb.astype(BF16), w_branch_ml.astype(BF16), gates)
    mix = _matmul(merged, w_out.astype(BF16), F32, "outproj")

    w_router = jnp.zeros((d, LANES), F32).at[:, :N_GROUPS].set(w_rg).at[:, N_GROUPS:N_GROUPS + n_exp].set(w_re)
    b_router = jnp.zeros((1, LANES), F32).at[0, :N_GROUPS].set(b_rg).at[0, N_GROUPS:N_GROUPS + n_exp].set(b_re)
    x1, u2p, logits = _mid(mix, x2, gate1, ln1_g.reshape(1, d), ln1_b.reshape(1, d), scale2, shift2,
                           w_router, b_router)

    pos, wf, cnt = _route(logits)
    pos1, pos2 = pos[:, 0], pos[:, 1]
    counts = cnt[0, N_GROUPS:N_GROUPS + n_exp].astype(I32)
    tiles_per_expert = (counts + MOE_TILE - 1) // MOE_TILE
    tile_end = jnp.cumsum(tiles_per_expert)
    n_tiles = -(-(s * TOP_K_IN_GROUP) // MOE_TILE) + n_exp
    tile_expert = jnp.minimum(
        jnp.sum((tile_end[None, :] <= jnp.arange(n_tiles, dtype=I32)[:, None]).astype(I32), axis=1), n_exp - 1)
    n_active = tile_end[-1:]
    tail_tiles = n_active + jnp.arange(n_tiles - s * TOP_K_IN_GROUP // MOE_TILE, dtype=I32)
    zero_start = jnp.concatenate([jnp.where(counts > 0, (tile_end - 1) * MOE_TILE, -1),
                                  jnp.where(tail_tiles < n_tiles, tail_tiles * MOE_TILE, -1)])

    xs = _dispatch(pos1, pos2, zero_start, u2p, n_tiles * MOE_TILE)
    hid = _ffn_a(tile_expert, n_active, xs, w1, w3)
    y = _ffn_b(tile_expert, n_active, hid, w2)
    return _combine(pos1, pos2, y, x1, wf, gate2, ln2_g.reshape(1, d), ln2_b.reshape(1, d))


def kernel(x, c, w_ada, b_ada, w_in, b_if, conv_w, conv_b, head_norm_g, w_branch_sb, w_branch_ml, w_out,
           ln1_g, ln1_b, w_rg, b_rg, w_re, b_re, w1, w3, w2, ln2_g, ln2_b):
    bsz, seq, d = x.shape
    assert bsz == 1 and w_ada.shape[0] == DEPTH
    x2 = x.reshape(seq, d)
    for l in range(DEPTH):
        x2 = _layer(x2, c, w_ada[l], b_ada[l], w_in[l], b_if[l], conv_w[l], conv_b[l], head_norm_g[l],
                    w_branch_sb[l], w_branch_ml[l], w_out[l], ln1_g[l], ln1_b[l], w_rg[l], b_rg[l],
                    w_re[l], b_re[l], w1[l], w3[l], w2[l], ln2_g[l], ln2_b[l])
    return x2.reshape(bsz, seq, d)
```

```python
import functools

import jax
import jax.numpy as jnp
from jax import lax
from jax.experimental import pallas as pl
from jax.experimental.pallas import tpu as pltpu

F32 = jnp.float32
BF16 = jnp.bfloat16
I32 = jnp.int32

DEPTH = 1
SB_HEADS = 16
SB_HEAD_DIM = 128
ML_HEADS = 8
ML_V_DIM = 256
ML_QK_DIM = 128
CONV_WIDTH = 4
N_GROUPS = 4
EXPERTS_PER_GROUP = 8
TOP_K_IN_GROUP = 2
N_MOD = 6
DN_ALPHA = (2.0 * DEPTH) ** 0.25
LN_EPS = 1e-5

LANES = 128
SUBLANES = 8
VMEM_LIMIT_BYTES = 60 * 1024 * 1024

EXP_UNDERFLOW = -104.0
OUT_OF_RANGE = -1e30

MOE_TILE = 256
ATTN_HEADS_PER_STEP = 2
ATTN_WALK_UNROLL = 2


def _tile(n, pref, align=LANES):
    if n <= pref:
        return n
    t = (pref // align) * align
    while n % t:
        t -= align
    return t


def _cparams(*sem):
    return pltpu.CompilerParams(dimension_semantics=tuple(sem), vmem_limit_bytes=VMEM_LIMIT_BYTES)


def _ln(x):
    mu = jnp.mean(x, axis=-1, keepdims=True)
    xc = x - mu
    var = jnp.mean(xc * xc, axis=-1, keepdims=True)
    return xc * lax.rsqrt(var + LN_EPS)


def _split2(x):
    hi = x.astype(BF16)
    lo = (x - hi.astype(F32)).astype(BF16)
    return hi, lo


def _split3(x):
    hi = x.astype(BF16)
    r = x - hi.astype(F32)
    mid = r.astype(BF16)
    lo = (r - mid.astype(F32)).astype(BF16)
    return hi, mid, lo


def _dot(a, b):
    return jnp.dot(a, b, preferred_element_type=F32)


def _dot_nt(a, b):
    return lax.dot_general(a, b, (((1,), (1,)), ((), ())), preferred_element_type=F32)


def _dot_tn(a, b):
    return lax.dot_general(a, b, (((0,), (0,)), ((), ())), preferred_element_type=F32)


def _softplus(z):
    return jnp.maximum(z, 0.0) + jnp.log(1.0 + jnp.exp(-jnp.abs(z)))


def _mod_kernel(c_ref, w_ref, b_ref, o_ref):
    @pl.when(pl.program_id(1) == 0)
    def _():
        o_ref[...] = b_ref[...]

    c = c_ref[...]
    s = c * jax.nn.sigmoid(c)
    o_ref[...] += jnp.sum(w_ref[...] * s, axis=0, keepdims=True)


def _mod(c_col, w_ada, b_ada):
    d, n = w_ada.shape
    tk = _tile(d, 512, SUBLANES)
    tn = _tile(n, 2048)
    return pl.pallas_call(
        _mod_kernel,
        grid=(n // tn, d // tk),
        in_specs=[pl.BlockSpec((tk, 1), lambda j, k: (k, 0)),
                  pl.BlockSpec((tk, tn), lambda j, k: (k, j)),
                  pl.BlockSpec((1, tn), lambda j, k: (0, j))],
        out_specs=pl.BlockSpec((1, tn), lambda j, k: (0, j)),
        out_shape=jax.ShapeDtypeStruct((1, n), F32),
        compiler_params=_cparams("parallel", "arbitrary"),
        name="mod",
    )(c_col, w_ada, b_ada)


def _ln1_kernel(x_ref, sc_ref, sh_ref, wh_ref, wl_ref, wth_ref, wtl_ref, u_ref, if_ref, ift_ref):
    u = _ln(x_ref[...]) * (1.0 + sc_ref[...]) + sh_ref[...]
    u_ref[...] = u.astype(BF16)
    uh, ul = _split2(u)
    if_ref[...] = _dot(uh, wh_ref[...]) + _dot(ul, wh_ref[...]) + _dot(uh, wl_ref[...])
    ift_ref[...] = (_dot_nt(wth_ref[...], uh) + _dot_nt(wth_ref[...], ul) + _dot_nt(wtl_ref[...], uh))


def _ln1(x2, scale, shift, w_if_t):
    s, d = x2.shape
    tm = _tile(s, 256, SUBLANES)
    wpad_t = jnp.zeros((LANES, d), F32).at[:w_if_t.shape[0], :].set(w_if_t)
    wth, wtl = _split2(wpad_t)
    wh, wl = wth.T, wtl.T
    row = pl.BlockSpec((1, d), lambda i: (0, 0))
    wspec = pl.BlockSpec((d, LANES), lambda i: (0, 0))
    wtspec = pl.BlockSpec((LANES, d), lambda i: (0, 0))
    return pl.pallas_call(
        _ln1_kernel,
        grid=(s // tm,),
        in_specs=[pl.BlockSpec((tm, d), lambda i: (i, 0)), row, row, wspec, wspec, wtspec, wtspec],
        out_specs=[pl.BlockSpec((tm, d), lambda i: (i, 0)),
                   pl.BlockSpec((tm, LANES), lambda i: (i, 0)),
                   pl.BlockSpec((LANES, tm), lambda i: (0, i))],
        out_shape=[jax.ShapeDtypeStruct((s, d), BF16),
                   jax.ShapeDtypeStruct((s, LANES), F32),
                   jax.ShapeDtypeStruct((LANES, s), F32)],
        compiler_params=_cparams("parallel"),
        name="ln1",
    )(x2, scale, shift, wh, wl, wth, wtl)


def _mm_kernel(a_ref, b_ref, o_ref):
    o_ref[...] = _dot(a_ref[...], b_ref[...]).astype(o_ref.dtype)


def _matmul(a, b, out_dtype, name, col0=0, ncols=None):
    m, k = a.shape
    n = b.shape[1] if ncols is None else ncols
    tm = _tile(m, 1024, SUBLANES)
    tn = _tile(n, 1024)
    j0 = col0 // tn
    assert j0 * tn == col0
    return pl.pallas_call(
        _mm_kernel,
        grid=(m // tm, n // tn),
        in_specs=[pl.BlockSpec((tm, k), lambda i, j: (i, 0)),
                  pl.BlockSpec((k, tn), lambda i, j: (0, j0 + j))],
        out_specs=pl.BlockSpec((tm, tn), lambda i, j: (i, j)),
        out_shape=jax.ShapeDtypeStruct((m, n), out_dtype),
        compiler_params=_cparams("parallel", "arbitrary"),
        name=name,
    )(a, b)


def _mm_wt_kernel(a_ref, w_ref, o_ref, wb):
    @pl.when(pl.program_id(1) == 0)
    def _():
        wb[...] = w_ref[...].astype(BF16).T

    o_ref[...] = _dot(a_ref[...], wb[...]).astype(o_ref.dtype)


def _matmul_wt(a, wt, row0, nrows, out_dtype, name):
    m, k = a.shape
    tm = _tile(m, 512, SUBLANES)
    tn = _tile(nrows, 1024)
    assert row0 % SUBLANES == 0
    return pl.pallas_call(
        _mm_wt_kernel,
        grid=(nrows // tn, m // tm),
        in_specs=[pl.BlockSpec((tm, k), lambda j, i: (i, 0)),
                  pl.BlockSpec((pl.Element(tn), pl.Element(k)), lambda j, i: (pl.multiple_of(row0 + j * tn, SUBLANES), 0))],
        out_specs=pl.BlockSpec((tm, tn), lambda j, i: (i, j)),
        out_shape=jax.ShapeDtypeStruct((m, nrows), out_dtype),
        scratch_shapes=[pltpu.VMEM((k, tn), BF16)],
        compiler_params=_cparams("arbitrary", "arbitrary"),
        name=name,
    )(a, wt)


def _attn_kernel(q_ref, k_ref, v_ref, o_ref, *, tq, scale):
    qb = pl.program_id(1)
    dh = SB_HEAD_DIM
    nh = q_ref.shape[1] // dh
    row = lax.broadcasted_iota(I32, (tq, tq), 0)
    col = lax.broadcasted_iota(I32, (tq, tq), 1)
    strict = col < row
    later = (row > col).astype(BF16)
    qs = [q_ref[:, i * dh:(i + 1) * dh] for i in range(nh)]

    def tile(i, kb, c, diagonal):
        start = pl.multiple_of(kb * tq, tq)
        k = k_ref[pl.ds(start, tq), i * dh:(i + 1) * dh]
        v = v_ref[pl.ds(start, tq), i * dh:(i + 1) * dh]
        z = _dot_nt(qs[i], k) * scale
        log_keep = -_softplus(z)
        if diagonal:
            log_keep = jnp.where(strict, log_keep, 0.0)
        hi, lo = _split2(log_keep)
        between = _dot(hi, later) + _dot(lo, later) + c
        w = jnp.exp(z + log_keep + between)
        if diagonal:
            w = jnp.where(strict, w, 0.0)
        pv = _dot(w.astype(BF16), v)
        return pv, c + jnp.sum(log_keep, axis=1, keepdims=True)

    zero_c = jnp.zeros((tq, 1), F32)
    first = [tile(i, qb, zero_c, True) for i in range(nh)]
    accs = tuple(p for p, _ in first)
    cs = tuple(c for _, c in first)

    def cmax_of(cs):
        m = jnp.max(cs[0])
        for c in cs[1:]:
            m = jnp.maximum(m, jnp.max(c))
        return m

    def cond(st):
        j, _, _, cmax = st
        return jnp.logical_and(j <= qb, cmax > EXP_UNDERFLOW)

    def body(st):
        j, accs, cs, _ = st
        accs, cs = list(accs), list(cs)
        for u in range(ATTN_WALK_UNROLL):
            kb = qb - j - u
            in_range = kb >= 0
            kb = jnp.maximum(kb, 0)
            for i in range(nh):
                c_in = cs[i] if u == 0 else jnp.where(in_range, cs[i], OUT_OF_RANGE)
                pv, c_out = tile(i, kb, c_in, False)
                accs[i] = accs[i] + pv
                cs[i] = c_out
        return j + ATTN_WALK_UNROLL, tuple(accs), tuple(cs), cmax_of(cs)

    _, accs, _, _ = lax.while_loop(cond, body, (jnp.int32(1), accs, cs, cmax_of(cs)))
    for i in range(nh):
        o_ref[:, i * dh:(i + 1) * dh] = accs[i].astype(o_ref.dtype)


def _attention(qkv):
    s = qkv.shape[0]
    tq = _tile(s, 256, SUBLANES)
    dh = SB_HEAD_DIM
    hps = ATTN_HEADS_PER_STEP
    groups = SB_HEADS // hps
    w = hps * dh
    kern = functools.partial(_attn_kernel, tq=tq, scale=dh ** -0.5)
    return pl.pallas_call(
        kern,
        grid=(groups, s // tq),
        in_specs=[pl.BlockSpec((tq, w), lambda g, i: (i, g)),
                  pl.BlockSpec((s, w), lambda g, i: (0, groups + g)),
                  pl.BlockSpec((s, w), lambda g, i: (0, 2 * groups + g))],
        out_specs=pl.BlockSpec((tq, w), lambda g, i: (i, g)),
        out_shape=jax.ShapeDtypeStruct((s, SB_HEADS * dh), BF16),
        compiler_params=_cparams("parallel", "arbitrary"),
        name="attn",
    )(qkv, qkv, qkv)


def _mlstm_kernel(qk_ref, v_ref, o_ref, if_ref, ift_ref, cw_ref, cb_ref, brow_ref, bcol_ref, hg_ref,
                  out_ref, xbuf, ct_ref, nt_ref, m_ref, *, tr):
    heads, dk, dv = ML_HEADS, ML_QK_DIM, ML_V_DIM
    qkw = heads * dk
    halo = SUBLANES

    @pl.when(pl.program_id(0) == 0)
    def _():
        xbuf[0:halo, :] = jnp.zeros((halo, 2 * qkw), F32)
        ct_ref[...] = jnp.zeros_like(ct_ref)
        nt_ref[...] = jnp.zeros_like(nt_ref)
        m_ref[...] = jnp.zeros_like(m_ref)

    xbuf[halo:halo + tr, :] = qk_ref[...]
    y = cb_ref[...]
    for i in range(CONV_WIDTH):
        y = y + cw_ref[i:i + 1, :] * xbuf[pl.ds(halo - (CONV_WIDTH - 1) + i, tr), :]
    xbuf[0:halo, :] = xbuf[tr:tr + halo, :]
    qk = y * jax.nn.sigmoid(y)

    g_col = if_ref[...] + brow_ref[...]
    g_row = ift_ref[...] + bcol_ref[...]
    lf_col = -_softplus(-g_col)
    lf_row = -_softplus(-g_row)
    r = lax.broadcasted_iota(I32, (tr, tr), 0)
    s = lax.broadcasted_iota(I32, (tr, tr), 1)
    causal = s <= r
    incl_col = causal.astype(BF16)
    incl_row = (r <= s).astype(BF16)
    c1, c2, c3 = _split3(lf_col)
    b_col_all = _dot(incl_col, c1) + _dot(incl_col, c2) + _dot(incl_col, c3)
    r1, r2, r3 = _split3(lf_row)
    b_row_all = _dot(r1, incl_row) + _dot(r2, incl_row) + _dot(r3, incl_row)

    for h in range(heads):
        q = qk[:, h * dk:(h + 1) * dk].astype(BF16)
        k = (qk[:, qkw + h * dk:qkw + (h + 1) * dk] * (dk ** -0.5)).astype(BF16)
        v = v_ref[:, h * dv:(h + 1) * dv]
        ig_col = g_col[:, h:h + 1]
        b_col = b_col_all[:, heads + h:heads + h + 1]
        ig_row = g_row[h:h + 1, :]
        b_row = b_row_all[heads + h:heads + h + 1, :]
        g_tot = b_row[:, tr - 1:tr]
        m_prev = m_ref[h][:, 0:1]

        dmat = jnp.where(causal, b_col - b_row + ig_row, -jnp.inf)
        m_inter = b_col + m_prev
        m_row = jnp.maximum(m_inter, jnp.max(dmat, axis=1, keepdims=True))
        p = jnp.exp(dmat - m_row) * _dot_nt(q, k)
        inter_scale = jnp.exp(m_inter - m_row)
        inter_num = _dot(q, ct_ref[h].astype(BF16))
        inter_den = _dot(q, nt_ref[h].astype(BF16))[:, 0:1]
        num = _dot(p.astype(BF16), v) + inter_scale * inter_num
        den = jnp.sum(p, axis=1, keepdims=True) + inter_scale * inter_den
        hh = num / jnp.maximum(jnp.abs(den), jnp.exp(-m_row))

        a_row = g_tot - b_row + ig_row
        a_col = g_tot - b_col + ig_col
        m_new = jnp.maximum(g_tot + m_prev, jnp.max(a_row, axis=1, keepdims=True))
        decay = jnp.exp(g_tot + m_prev - m_new)
        w_col = jnp.exp(a_col - m_new)
        ct_ref[h] = decay * ct_ref[h] + _dot_tn(k, (w_col * v.astype(F32)).astype(BF16))
        nt_ref[h] = decay * nt_ref[h] + _dot_tn(k, jnp.broadcast_to(w_col, (tr, LANES)).astype(BF16))
        m_ref[h] = jnp.broadcast_to(m_new, (1, LANES))

        hn = _ln(hh) * hg_ref[:, h * dv:(h + 1) * dv]
        gate = jax.nn.sigmoid(o_ref[:, h * dv:(h + 1) * dv].astype(F32))
        out_ref[:, h * dv:(h + 1) * dv] = (gate * hn).astype(out_ref.dtype)


def _mlstm(qk_pre, vo, if_col, if_row, conv_w, conv_b, b_if, head_norm_g):
    s = qk_pre.shape[0]
    tr = _tile(s, 256, SUBLANES)
    heads, dk, dv = ML_HEADS, ML_QK_DIM, ML_V_DIM
    mlw = heads * dv
    qkw2 = 2 * heads * dk
    brow = jnp.zeros((1, LANES), F32).at[0, :2 * heads].set(b_if)
    full = lambda shape: pl.BlockSpec(shape, lambda i: (0,) * len(shape))
    kern = functools.partial(_mlstm_kernel, tr=tr)
    return pl.pallas_call(
        kern,
        grid=(s // tr,),
        in_specs=[pl.BlockSpec((tr, qkw2), lambda i: (i, 0)),
                  pl.BlockSpec((tr, mlw), lambda i: (i, 0)),
                  pl.BlockSpec((tr, mlw), lambda i: (i, 1)),
                  pl.BlockSpec((tr, LANES), lambda i: (i, 0)),
                  pl.BlockSpec((LANES, tr), lambda i: (0, i)),
                  full((CONV_WIDTH, qkw2)), full((1, qkw2)), full((1, LANES)), full((LANES, 1)),
                  full((1, mlw))],
        out_specs=pl.BlockSpec((tr, mlw), lambda i: (i, 0)),
        out_shape=jax.ShapeDtypeStruct((s, mlw), BF16),
        scratch_shapes=[pltpu.VMEM((tr + 2 * SUBLANES, qkw2), F32),
                        pltpu.VMEM((heads, dk, dv), F32),
                        pltpu.VMEM((heads, dk, LANES), F32),
                        pltpu.VMEM((heads, 1, LANES), F32)],
        compiler_params=_cparams("arbitrary"),
        name="mlstm",
    )(qk_pre, vo, vo, if_col, if_row, conv_w, conv_b.reshape(1, -1), brow, brow.reshape(LANES, 1),
      head_norm_g.reshape(1, -1))


def _merge_kernel(y_ref, h_ref, wsb_ref, wml_ref, gsb_ref, gml_ref, o_ref):
    a = _dot(y_ref[...], wsb_ref[...])
    b = _dot(h_ref[...], wml_ref[...])
    o = jax.nn.sigmoid(gsb_ref[...].astype(F32)) * a + jax.nn.sigmoid(gml_ref[...].astype(F32)) * b
    o_ref[...] = o.astype(o_ref.dtype)


def _merge(y_sb, h_ml, w_sb, w_ml, gates):
    s = y_sb.shape[0]
    d = w_sb.shape[1]
    tm = _tile(s, 512, SUBLANES)
    tn = _tile(d, 1024)
    nj = d // tn
    return pl.pallas_call(
        _merge_kernel,
        grid=(s // tm, nj),
        in_specs=[pl.BlockSpec((tm, y_sb.shape[1]), lambda i, j: (i, 0)),
                  pl.BlockSpec((tm, h_ml.shape[1]), lambda i, j: (i, 0)),
                  pl.BlockSpec((w_sb.shape[0], tn), lambda i, j: (0, j)),
                  pl.BlockSpec((w_ml.shape[0], tn), lambda i, j: (0, j)),
                  pl.BlockSpec((tm, tn), lambda i, j: (i, j)),
                  pl.BlockSpec((tm, tn), lambda i, j: (i, nj + j))],
        out_specs=pl.BlockSpec((tm, tn), lambda i, j: (i, j)),
        out_shape=jax.ShapeDtypeStruct((s, d), BF16),
        compiler_params=_cparams("parallel", "arbitrary"),
        name="merge",
    )(y_sb, h_ml, w_sb, w_ml, gates, gates)


def _mid_kernel(mix_ref, x_ref, g1_ref, lng_ref, lnb_ref, sc2_ref, sh2_ref, wrh_ref, wrl_ref, br_ref,
                x1_ref, u2_ref, lg_ref):
    x1 = _ln(DN_ALPHA * x_ref[...] + g1_ref[...] * mix_ref[...]) * lng_ref[...] + lnb_ref[...]
    x1_ref[...] = x1
    u2 = _ln(x1) * (1.0 + sc2_ref[...]) + sh2_ref[...]
    u2_ref[...] = u2.astype(BF16)
    uh, ul = _split2(u2)
    lg_ref[...] = (_dot(uh, wrh_ref[...]) + _dot(ul, wrh_ref[...]) + _dot(uh, wrl_ref[...]) + br_ref[...])


def _mid(mix, x2, gate1, ln_g, ln_b, scale2, shift2, w_router, b_router):
    s, d = x2.shape
    tm = _tile(s, 256, SUBLANES)
    wrh, wrl = _split2(w_router)
    row = pl.BlockSpec((1, d), lambda i: (0, 0))
    wr = pl.BlockSpec((d, LANES), lambda i: (0, 0))
    tile = pl.BlockSpec((tm, d), lambda i: (i, 0))
    return pl.pallas_call(
        _mid_kernel,
        grid=(s // tm,),
        in_specs=[tile, tile, row, row, row, row, row, wr, wr, pl.BlockSpec((1, LANES), lambda i: (0, 0))],
        out_specs=[tile, tile, pl.BlockSpec((tm, LANES), lambda i: (i, 0))],
        out_shape=[jax.ShapeDtypeStruct((s, d), F32),
                   jax.ShapeDtypeStruct((s, d), BF16),
                   jax.ShapeDtypeStruct((s, LANES), F32)],
        compiler_params=_cparams("parallel"),
        name="mid",
    )(mix, x2, gate1, ln_g, ln_b, scale2, shift2, wrh, wrl, b_router)


def _route_kernel(lg_ref, pos_ref, wf_ref, cnt_ref, carry_ref, pstart_ref, *, tm):
    phase = pl.program_id(0)
    i = pl.program_id(1)
    ng, epg = N_GROUPS, EXPERTS_PER_GROUP

    @pl.when(jnp.logical_and(phase == 0, i == 0))
    def _():
        carry_ref[...] = jnp.zeros_like(carry_ref)

    @pl.when(jnp.logical_and(phase == 1, i == 0))
    def _():
        counts = carry_ref[...]
        cnt_ref[...] = counts
        n_tiles = jnp.ceil(counts * (1.0 / MOE_TILE))
        a = lax.broadcasted_iota(I32, (LANES, LANES), 0)
        b = lax.broadcasted_iota(I32, (LANES, LANES), 1)
        tiles_before = _dot(jnp.broadcast_to(n_tiles, (SUBLANES, LANES)).astype(BF16), (a < b).astype(BF16))
        pstart_ref[...] = tiles_before[0:1, :] * float(MOE_TILE)
        carry_ref[...] = jnp.zeros_like(carry_ref)

    lg = lg_ref[...]
    lane = lax.broadcasted_iota(I32, lg.shape, 1)
    lane_f = lane.astype(F32)
    neg = -jnp.inf
    big = float(LANES)

    def first_argmax(vals, vmax):
        return jnp.min(jnp.where(vals == vmax, lane_f, big), axis=1, keepdims=True)

    gmask = lane < ng
    gl = jnp.where(gmask, lg, neg)
    gmax = jnp.max(gl, axis=1, keepdims=True)
    gidx = first_argmax(gl, gmax).astype(I32)
    lo = ng + gidx * epg
    el = jnp.where(jnp.logical_and(lane >= lo, lane < lo + epg), lg, neg)
    t1 = jnp.max(el, axis=1, keepdims=True)
    i1 = first_argmax(el, t1).astype(I32)
    el2 = jnp.where(lane == i1, neg, el)
    t2 = jnp.max(el2, axis=1, keepdims=True)
    i2 = first_argmax(el2, t2).astype(I32)
    hit1 = lane == i1
    hit2 = lane == i2
    onehot = jnp.logical_or(hit1, hit2)

    @pl.when(phase == 1)
    def _():
        gsum = jnp.sum(jnp.where(gmask, jnp.exp(lg - gmax), 0.0), axis=1, keepdims=True)
        g_w = 1.0 / gsum
        ex = jnp.exp(t2 - t1)
        w1 = g_w / (1.0 + ex)
        w2 = g_w * ex / (1.0 + ex)
        r = lax.broadcasted_iota(I32, (tm, tm), 0)
        c = lax.broadcasted_iota(I32, (tm, tm), 1)
        row_of = (_dot((c < r).astype(BF16), onehot.astype(BF16)) + carry_ref[...] + pstart_ref[...])
        pos1 = jnp.sum(jnp.where(hit1, row_of, 0.0), axis=1, keepdims=True).astype(I32)
        pos2 = jnp.sum(jnp.where(hit2, row_of, 0.0), axis=1, keepdims=True).astype(I32)
        pos_ref[...] = jnp.where(lane == 0, pos1, pos2)[:, :SUBLANES]
        wf_ref[...] = jnp.where(lane == 0, w1, w2)[:, :SUBLANES]

    carry_ref[...] += jnp.sum(onehot.astype(F32), axis=0, keepdims=True)


def _route(logits):
    s = logits.shape[0]
    tm = _tile(s, 512, SUBLANES)
    kern = functools.partial(_route_kernel, tm=tm)
    return pl.pallas_call(
        kern,
        grid=(2, s // tm),
        in_specs=[pl.BlockSpec((tm, LANES), lambda p, i: (i, 0))],
        out_specs=[pl.BlockSpec((tm, SUBLANES), lambda p, i: (i * p, 0)),
                   pl.BlockSpec((tm, SUBLANES), lambda p, i: (i * p, 0)),
                   pl.BlockSpec((1, LANES), lambda p, i: (0, 0))],
        out_shape=[jax.ShapeDtypeStruct((s, SUBLANES), I32),
                   jax.ShapeDtypeStruct((s, SUBLANES), F32),
                   jax.ShapeDtypeStruct((1, LANES), F32)],
        scratch_shapes=[pltpu.VMEM((1, LANES), F32), pltpu.VMEM((1, LANES), F32)],
        compiler_params=_cparams("arbitrary", "arbitrary"),
        name="route",
    )(logits)


def _dispatch_kernel(p1_ref, p2_ref, zs_ref, x_ref, dst_ref, stage, zbuf, sem, zsem, *, tm, n_zero):
    i = pl.program_id(0)
    n = pl.num_programs(0)
    slot = i % 2

    def zero_copy(e):
        start = pl.multiple_of(zs_ref[e], MOE_TILE)
        return pltpu.make_async_copy(zbuf, dst_ref.at[pl.ds(start, MOE_TILE)], zsem)

    def drain(s):
        for _ in range(TOP_K_IN_GROUP):
            pltpu.make_async_copy(stage.at[s], dst_ref.at[pl.ds(0, tm)], sem.at[s]).wait()

    @pl.when(i == 0)
    def _():
        zbuf[...] = jnp.zeros_like(zbuf)
        for e in range(n_zero):
            pl.when(zs_ref[e] >= 0)(lambda e=e: zero_copy(e).start())
        for e in range(n_zero):
            pl.when(zs_ref[e] >= 0)(lambda e=e: zero_copy(e).wait())

    @pl.when(i >= 2)
    def _():
        drain(slot)

    stage[slot] = x_ref[...].astype(F32)

    def body(r, carry):
        t = i * tm + r
        src = stage.at[slot, pl.ds(r, 1)]
        pltpu.make_async_copy(src, dst_ref.at[pl.ds(p1_ref[t], 1)], sem.at[slot]).start()
        pltpu.make_async_copy(src, dst_ref.at[pl.ds(p2_ref[t], 1)], sem.at[slot]).start()
        return carry

    lax.fori_loop(0, tm, body, 0, unroll=8)

    @pl.when(i == n - 1)
    def _():
        drain(slot)

        @pl.when(n >= 2)
        def _():
            drain(1 - slot)


def _dispatch(pos1, pos2, zero_start, u2p, n_rows):
    s, w = u2p.shape
    tm = _tile(s, 256, SUBLANES)
    kern = functools.partial(_dispatch_kernel, tm=tm, n_zero=zero_start.shape[0])
    return pl.pallas_call(
        kern,
        grid_spec=pltpu.PrefetchScalarGridSpec(
            num_scalar_prefetch=3,
            grid=(s // tm,),
            in_specs=[pl.BlockSpec((tm, w), lambda i, p1, p2, zs: (i, 0))],
            out_specs=pl.BlockSpec(memory_space=pl.ANY),
            scratch_shapes=[pltpu.VMEM((2, tm, w), F32), pltpu.VMEM((MOE_TILE, w), F32),
                            pltpu.SemaphoreType.DMA((2,)), pltpu.SemaphoreType.DMA]),
        out_shape=jax.ShapeDtypeStruct((n_rows, w), F32),
        compiler_params=_cparams("arbitrary"),
        name="dispatch",
    )(pos1, pos2, zero_start, u2p)


def _ffn_a_kernel(te_ref, na_ref, x_ref, w1_ref, w3_ref, o_ref, w1b, w3b):
    t = pl.program_id(1)

    @pl.when(t < na_ref[0])
    def _():
        changed = jnp.logical_or(t == 0, te_ref[t] != te_ref[jnp.maximum(t - 1, 0)])

        @pl.when(changed)
        def _():
            w1b[...] = w1_ref[0].astype(BF16)
            w3b[...] = w3_ref[0].astype(BF16)

        x = x_ref[...].astype(BF16)
        a = _dot(x, w1b[...])
        b = _dot(x, w3b[...])
        o_ref[...] = (a * jax.nn.sigmoid(a) * b).astype(o_ref.dtype)

    @pl.when(t >= na_ref[0])
    def _():
        o_ref[...] = jnp.zeros_like(o_ref)


def _ffn_a(tile_expert, n_active, xs, w1, w3):
    n_rows, d = xs.shape
    de = w1.shape[2]
    ck = _tile(de, 512)
    n_tiles = n_rows // MOE_TILE

    def row_idx(c, t, te, na):
        return (jnp.minimum(t, na[0] - 1), 0)

    def w_idx(c, t, te, na):
        return (te[jnp.minimum(t, na[0] - 1)], 0, c)

    wspec = pl.BlockSpec((1, d, ck), w_idx)
    return pl.pallas_call(
        _ffn_a_kernel,
        grid_spec=pltpu.PrefetchScalarGridSpec(
            num_scalar_prefetch=2,
            grid=(de // ck, n_tiles),
            in_specs=[pl.BlockSpec((MOE_TILE, d), row_idx), wspec, wspec],
            out_specs=pl.BlockSpec((MOE_TILE, ck), lambda c, t, te, na: (t, c)),
            scratch_shapes=[pltpu.VMEM((d, ck), BF16), pltpu.VMEM((d, ck), BF16)]),
        out_shape=jax.ShapeDtypeStruct((n_rows, de), BF16),
        compiler_params=_cparams("arbitrary", "arbitrary"),
        name="ffn_a",
    )(tile_expert, n_active, xs, w1, w3)


def _ffn_b_kernel(te_ref, na_ref, h_ref, w2_ref, o_ref, w2b):
    t = pl.program_id(0)

    @pl.when(t < na_ref[0])
    def _():
        changed = jnp.logical_or(t == 0, te_ref[t] != te_ref[jnp.maximum(t - 1, 0)])

        @pl.when(changed)
        def _():
            w2b[...] = w2_ref[0].astype(BF16)

        o_ref[...] = _dot(h_ref[...], w2b[...])

    @pl.when(t >= na_ref[0])
    def _():
        o_ref[...] = jnp.zeros_like(o_ref)


def _ffn_b(tile_expert, n_active, hid, w2):
    n_rows, de = hid.shape
    d = w2.shape[2]
    n_tiles = n_rows // MOE_TILE

    def row_idx(t, te, na):
        return (jnp.minimum(t, na[0] - 1), 0)

    def w_idx(t, te, na):
        return (te[jnp.minimum(t, na[0] - 1)], 0, 0)

    return pl.pallas_call(
        _ffn_b_kernel,
        grid_spec=pltpu.PrefetchScalarGridSpec(
            num_scalar_prefetch=2,
            grid=(n_tiles,),
            in_specs=[pl.BlockSpec((MOE_TILE, de), row_idx),
                      pl.BlockSpec((1, de, d), w_idx)],
            out_specs=pl.BlockSpec((MOE_TILE, d), lambda t, te, na: (t, 0)),
            scratch_shapes=[pltpu.VMEM((de, d), BF16)]),
        out_shape=jax.ShapeDtypeStruct((n_rows, d), F32),
        compiler_params=_cparams("arbitrary"),
        name="ffn_b",
    )(tile_expert, n_active, hid, w2)


def _combine_kernel(p1_ref, p2_ref, y_ref, x1_ref, wf_ref, g2_ref, lng_ref, lnb_ref, o_ref, ybuf, sem, *, tm):
    i = pl.program_id(0)
    n = pl.num_programs(0)

    def issue(tile, slot):
        def body(r, carry):
            t = tile * tm + r
            pltpu.make_async_copy(y_ref.at[pl.ds(p1_ref[t], 1)], ybuf.at[slot, 0, pl.ds(r, 1)],
                                  sem.at[slot]).start()
            pltpu.make_async_copy(y_ref.at[pl.ds(p2_ref[t], 1)], ybuf.at[slot, 1, pl.ds(r, 1)],
                                  sem.at[slot]).start()
            return carry

        lax.fori_loop(0, tm, body, 0, unroll=8)

    @pl.when(i == 0)
    def _():
        issue(0, 0)

    @pl.when(i + 1 < n)
    def _():
        issue(i + 1, (i + 1) % 2)

    slot = i % 2
    for k in range(2):
        pltpu.make_async_copy(y_ref.at[pl.ds(0, tm)], ybuf.at[slot, k], sem.at[slot]).wait()

    wf = wf_ref[...]
    w1 = wf[:, 0:1]
    w2 = wf[:, 1:2]
    ffn = w1 * ybuf[slot, 0] + w2 * ybuf[slot, 1]
    o_ref[...] = _ln(DN_ALPHA * x1_ref[...] + g2_ref[...] * ffn) * lng_ref[...] + lnb_ref[...]


def _combine(pos1, pos2, y, x1, wf, gate2, ln_g, ln_b):
    s, d = x1.shape
    tm = _tile(s, 256, SUBLANES)
    row = pl.BlockSpec((1, d), lambda i, p1, p2: (0, 0))
    kern = functools.partial(_combine_kernel, tm=tm)
    return pl.pallas_call(
        kern,
        grid_spec=pltpu.PrefetchScalarGridSpec(
            num_scalar_prefetch=2,
            grid=(s // tm,),
            in_specs=[pl.BlockSpec(memory_space=pl.ANY),
                      pl.BlockSpec((tm, d), lambda i, p1, p2: (i, 0)),
                      pl.BlockSpec((tm, SUBLANES), lambda i, p1, p2: (i, 0)),
                      row, row, row],
            out_specs=pl.BlockSpec((tm, d), lambda i, p1, p2: (i, 0)),
            scratch_shapes=[pltpu.VMEM((2, 2, tm, d), F32), pltpu.SemaphoreType.DMA((2,))]),
        out_shape=jax.ShapeDtypeStruct((s, d), F32),
        compiler_params=_cparams("arbitrary"),
        name="combine",
    )(pos1, pos2, y, x1, wf, gate2, ln_g, ln_b)


def _layer(x2, c, w_ada, b_ada, w_in, b_if, conv_w, conv_b, head_norm_g, w_branch_sb, w_branch_ml, w_out,
           ln1_g, ln1_b, w_rg, b_rg, w_re, b_re, w1, w3, w2, ln2_g, ln2_b):
    s, d = x2.shape
    sbw = SB_HEADS * SB_HEAD_DIM
    mlw = ML_HEADS * ML_V_DIM
    qkw2 = 2 * ML_HEADS * ML_QK_DIM
    n_exp = N_GROUPS * EXPERTS_PER_GROUP

    mod = _mod(c.reshape(d, 1), w_ada, b_ada.reshape(1, -1))
    shift1, scale1, gate1, shift2, scale2, gate2 = [mod[:, i * d:(i + 1) * d] for i in range(N_MOD)]

    o_qk = 3 * sbw
    o_vo = o_qk + qkw2
    o_if = o_vo + 2 * mlw
    o_gates = o_if + 2 * ML_HEADS
    w_in_t = w_in.T

    u1, if_col, if_row = _ln1(x2, scale1, shift1, w_in_t[o_if:o_gates])
    qkv = _matmul_wt(u1, w_in_t, 0, 3 * sbw, BF16, "inproj_sb")
    qk_pre = _matmul_wt(u1, w_in_t, o_qk, qkw2, F32, "inproj_qk")
    vo = _matmul_wt(u1, w_in_t, o_vo, 2 * mlw, BF16, "inproj_vo")
    gates = _matmul_wt(u1, w_in_t, o_gates, 2 * d, BF16, "inproj_gates")

    y_sb = _attention(qkv)
    h_ml = _mlstm(qk_pre, vo, if_col, if_row, conv_w, conv_b, b_if, head_norm_g)
    merged = _merge(y_sb, h_ml, w_branch_sb.astype(BF16), w_branch_ml.astype(BF16), gates)
    mix = _matmul(merged, w_out.astype(BF16), F32, "outproj")

    w_router = jnp.zeros((d, LANES), F32).at[:, :N_GROUPS].set(w_rg).at[:, N_GROUPS:N_GROUPS + n_exp].set(w_re)
    b_router = jnp.zeros((1, LANES), F32).at[0, :N_GROUPS].set(b_rg).at[0, N_GROUPS:N_GROUPS + n_exp].set(b_re)
    x1, u2p, logits = _mid(mix, x2, gate1, ln1_g.reshape(1, d), ln1_b.reshape(1, d), scale2, shift2,
                           w_router, b_router)

    pos, wf, cnt = _route(logits)
    pos1, pos2 = pos[:, 0], pos[:, 1]
    counts = cnt[0, N_GROUPS:N_GROUPS + n_exp].astype(I32)
    tiles_per_expert = (counts + MOE_TILE - 1) // MOE_TILE
    tile_end = jnp.cumsum(tiles_per_expert)
    n_tiles = -(-(s * TOP_K_IN_GROUP) // MOE_TILE) + n_exp
    tile_expert = jnp.minimum(
        jnp.sum((tile_end[None, :] <= jnp.arange(n_tiles, dtype=I32)[:, None]).astype(I32), axis=1), n_exp - 1)
    n_active = tile_end[-1:]
    tail_tiles = n_active + jnp.arange(n_tiles - s * TOP_K_IN_GROUP // MOE_TILE, dtype=I32)
    zero_start = jnp.concatenate([jnp.where(counts > 0, (tile_end - 1) * MOE_TILE, -1),
                                  jnp.where(tail_tiles < n_tiles, tail_tiles * MOE_TILE, -1)])

    xs = _dispatch(pos1, pos2, zero_start, u2p, n_tiles * MOE_TILE)
    hid = _ffn_a(tile_expert, n_active, xs, w1, w3)
    y = _ffn_b(tile_expert, n_active, hid, w2)
    return _combine(pos1, pos2, y, x1, wf, gate2, ln2_g.reshape(1, d), ln2_b.reshape(1, d))


def kernel(x, c, w_ada, b_ada, w_in, b_if, conv_w, conv_b, head_norm_g, w_branch_sb, w_branch_ml, w_out,
           ln1_g, ln1_b, w_rg, b_rg, w_re, b_re, w1, w3, w2, ln2_g, ln2_b):
    bsz, seq, d = x.shape
    assert bsz == 1 and w_ada.shape[0] == DEPTH
    x2 = x.reshape(seq, d)
    for l in range(DEPTH):
        x2 = _layer(x2, c, w_ada[l], b_ada[l], w_in[l], b_if[l], conv_w[l], conv_b[l], head_norm_g[l],
                    w_branch_sb[l], w_branch_ml[l], w_out[l], ln1_g[l], ln1_b[l], w_rg[l], b_rg[l],
                    w_re[l], b_re[l], w1[l], w3[l], w2[l], ln2_g[l], ln2_b[l])
    return x2.reshape(bsz, seq, d)
```
